```python
import math
import jax, jax.numpy as jnp
from jax import lax
import numpy as np

D_MODEL = 2048
BATCH = 1
SEQ = 8192
DEPTH = 4
DEC_BATCH = 32
DEC_SEQ = 32
PAST_LEN = 1024

CHUNK = 64
N_HEADS = 16
HEAD_DIM = 64
ATT_WIDTH = N_HEADS * HEAD_DIM
SSM_WIDTH = D_MODEL // 2
SSM_GROUP = 16
N_GROUPS = SSM_WIDTH // SSM_GROUP
STATE_DIM = 64
D_FF = 5504
Q_BLOCK = 128
EPS = 1e-6
DT_MIN = 1e-3
DT_MAX = 1e-1
FORGET_BIAS = 2.0
SPLITS = (ATT_WIDTH, 2 * ATT_WIDTH, 3 * ATT_WIDTH, 3 * ATT_WIDTH + N_HEADS,
          3 * ATT_WIDTH + N_HEADS + SSM_WIDTH)
IN_WIDTH = 3 * ATT_WIDTH + N_HEADS + SSM_WIDTH + 2 * D_MODEL

kernel_name = 'fox_s5_gated_macaron_stream_step'

F32 = jnp.float32


def rmsnorm(x, g):
    xf = x.astype(F32)
    y = xf * lax.rsqrt(jnp.mean(xf * xf, axis=-1, keepdims=True) + EPS)
    return (y * g.astype(F32)).astype(x.dtype)


def swiglu(x, w_gate, w_up, w_down):
    return (jax.nn.silu(x @ w_gate) * (x @ w_up)) @ w_down


def fox_block(q, cq, qpos, k, v, ck, kpos):
    s = jnp.einsum('bqhd,bkhd->bhqk', q, k).astype(F32) * (HEAD_DIM ** -0.5)
    s = s + (jnp.swapaxes(cq, 1, 2)[:, :, :, None] - jnp.swapaxes(ck, 1, 2)[:, :, None, :])
    mask = kpos[None, :] <= qpos[:, None]
    s = jnp.where(mask[None, None], s, -jnp.inf)
    p = jax.nn.softmax(s, axis=-1)
    return jnp.einsum('bhqk,bkhd->bqhd', p.astype(v.dtype), v)


def fox_prompt(q, k, v, logf):
    B, S = q.shape[0], q.shape[1]
    c = jnp.cumsum(logf, axis=1)
    pos = jnp.arange(S)
    nb = S // Q_BLOCK
    qb = jnp.swapaxes(q.reshape(B, nb, Q_BLOCK, N_HEADS, HEAD_DIM), 0, 1)
    cqb = jnp.swapaxes(c.reshape(B, nb, Q_BLOCK, N_HEADS), 0, 1)
    pb = pos.reshape(nb, Q_BLOCK)
    out = lax.map(lambda a: fox_block(a[0], a[1], a[2], k, v, c, pos), (qb, cqb, pb))
    return jnp.swapaxes(out, 0, 1).reshape(B, S, N_HEADS, HEAD_DIM)


def fox_sample(q, k, v, logf, past_k, past_v, past_logf):
    P, T = past_k.shape[1], q.shape[1]
    kk = jnp.concatenate([past_k.astype(k.dtype), k], axis=1)
    vv = jnp.concatenate([past_v.astype(v.dtype), v], axis=1)
    c = jnp.cumsum(jnp.concatenate([past_logf.astype(F32), logf], axis=1), axis=1)
    kpos = jnp.arange(P + T)
    qpos = P + jnp.arange(T)
    return fox_block(q, c[:, P:], qpos, kk, vv, c, kpos)


def s5_discretise(lam_re, lam_im, log_dt, b_re, b_im):
    dt = jnp.exp(log_dt.astype(F32))[:, None]
    lr, li = lam_re.astype(F32), lam_im.astype(F32)
    mag = jnp.exp(lr * dt)
    a_re = mag * jnp.cos(li * dt)
    a_im = mag * jnp.sin(li * dt)
    nr, ni = a_re - 1.0, a_im
    den = lr * lr + li * li
    f_re = (nr * lr + ni * li) / den
    f_im = (ni * lr - nr * li) / den
    br, bi = b_re.astype(F32), b_im.astype(F32)
    bb_re = f_re[..., None] * br - f_im[..., None] * bi
    bb_im = f_re[..., None] * bi + f_im[..., None] * br
    return a_re, a_im, bb_re, bb_im


def s5_combine(e1, e2):
    a1r, a1i, b1r, b1i = e1
    a2r, a2i, b2r, b2i = e2
    return (a2r * a1r - a2i * a1i, a2r * a1i + a2i * a1r,
            a2r * b1r - a2i * b1i + b2r, a2r * b1i + a2i * b1r + b2i)


def s5_branch(s_in, lw, h0_re, h0_im):
    B, L = s_in.shape[0], s_in.shape[1]
    a_re, a_im, bb_re, bb_im = s5_discretise(lw['ssm_lam_re'], lw['ssm_lam_im'], lw['ssm_log_dt'],
                                             lw['ssm_b_re'], lw['ssm_b_im'])
    u = s_in.astype(F32).reshape(B, L, N_GROUPS, SSM_GROUP)
    bu_re = jnp.einsum('gpn,blgn->blgp', bb_re, u)
    bu_im = jnp.einsum('gpn,blgn->blgp', bb_im, u)
    h0r, h0i = h0_re.astype(F32), h0_im.astype(F32)
    bu_re = bu_re.at[:, 0].add(a_re * h0r - a_im * h0i)
    bu_im = bu_im.at[:, 0].add(a_re * h0i + a_im * h0r)
    ar = jnp.broadcast_to(a_re, bu_re.shape)
    ai = jnp.broadcast_to(a_im, bu_re.shape)
    _, _, hr, hi = lax.associative_scan(s5_combine, (ar, ai, bu_re, bu_im), axis=1)
    y = (jnp.einsum('gnp,blgp->blgn', lw['ssm_c_re'].astype(F32), hr)
         - jnp.einsum('gnp,blgp->blgn', lw['ssm_c_im'].astype(F32), hi))
    y = y.reshape(B, L, SSM_WIDTH) + lw['ssm_d'].astype(F32) * s_in.astype(F32)
    y = jax.nn.gelu(y).astype(s_in.dtype)
    y = y * jax.nn.sigmoid(y @ lw['ssm_w_glu'] + lw['ssm_b_glu'])
    return y, hr[:, -1], hi[:, -1]


def token_mixer(u, lw, past, h0_re, h0_im):
    B, L = u.shape[0], u.shape[1]
    z = u @ lw['w_in']
    q, k, v, f_logit, s_in, gates = jnp.split(z, SPLITS, axis=-1)
    q = q.reshape(B, L, N_HEADS, HEAD_DIM)
    k = k.reshape(B, L, N_HEADS, HEAD_DIM)
    v = v.reshape(B, L, N_HEADS, HEAD_DIM)
    logf = jax.nn.log_sigmoid((f_logit + lw['b_forget']).astype(F32))
    if past is None:
        o_att = fox_prompt(q, k, v, logf)
    else:
        o_att = fox_sample(q, k, v, logf, past[0], past[1], past[2])
    o_att = o_att.reshape(B, L, ATT_WIDTH)
    o_ssm, h_re, h_im = s5_branch(s_in, lw, h0_re, h0_im)
    g = jax.nn.sigmoid(gates + lw['b_gate'])
    g_att, g_ssm = jnp.split(g, 2, axis=-1)
    merged = g_att * (o_att @ lw['w_proj_attn']) + g_ssm * (o_ssm @ lw['w_proj_ssm'])
    return merged @ lw['w_out'], k, v, logf, h_re, h_im


def layer(x, lw, past, h0_re, h0_im):
    x = x + 0.5 * rmsnorm(swiglu(rmsnorm(x, lw['norm_f1_pre']), lw['ffn1_w_gate'], lw['ffn1_w_up'],
                                 lw['ffn1_w_down']), lw['norm_f1_post'])
    m, k, v, logf, h_re, h_im = token_mixer(rmsnorm(x, lw['norm_mix_pre']), lw, past, h0_re, h0_im)
    x = x + rmsnorm(m, lw['norm_mix_post'])
    x = x + 0.5 * rmsnorm(swiglu(rmsnorm(x, lw['norm_f2_pre']), lw['ffn2_w_gate'], lw['ffn2_w_up'],
                                 lw['ffn2_w_down']), lw['norm_f2_post'])
    return x, k, v, logf, h_re, h_im


def setup_inputs(seed: int = 0) -> dict:
    key = jax.random.key(seed)
    ks = jax.random.split(key, 40)
    n = lambda i, shape, scale: jax.random.normal(ks[i], shape, F32) * scale
    gain = lambda i: 1.0 + 0.01 * jax.random.normal(ks[i], (DEPTH, D_MODEL), F32)
    lam_im_base = jnp.broadcast_to(math.pi * jnp.arange(STATE_DIM, dtype=F32), (DEPTH, N_GROUPS, STATE_DIM))
    return {
        'x_prompt': n(0, (BATCH, SEQ, D_MODEL), 1.0),
        'x_sample': n(1, (DEC_BATCH, DEC_SEQ, D_MODEL), 1.0),
        'cache_k': n(2, (DEPTH, DEC_BATCH, PAST_LEN, N_HEADS, HEAD_DIM), 1.0),
        'cache_v': n(3, (DEPTH, DEC_BATCH, PAST_LEN, N_HEADS, HEAD_DIM), 1.0),
        'cache_logf': jax.nn.log_sigmoid(FORGET_BIAS + n(4, (DEPTH, DEC_BATCH, PAST_LEN, N_HEADS), 1.0)),
        'state_ssm_re': n(5, (DEPTH, DEC_BATCH, N_GROUPS, STATE_DIM), 0.1),
        'state_ssm_im': n(6, (DEPTH, DEC_BATCH, N_GROUPS, STATE_DIM), 0.1),
        'norm_f1_pre': gain(7), 'norm_f1_post': gain(8),
        'norm_mix_pre': gain(9), 'norm_mix_post': gain(10),
        'norm_f2_pre': gain(11), 'norm_f2_post': gain(12),
        'ffn1_w_gate': n(13, (DEPTH, D_MODEL, D_FF), D_MODEL ** -0.5),
        'ffn1_w_up': n(14, (DEPTH, D_MODEL, D_FF), D_MODEL ** -0.5),
        'ffn1_w_down': n(15, (DEPTH, D_FF, D_MODEL), D_FF ** -0.5),
        'ffn2_w_gate': n(16, (DEPTH, D_MODEL, D_FF), D_MODEL ** -0.5),
        'ffn2_w_up': n(17, (DEPTH, D_MODEL, D_FF), D_MODEL ** -0.5),
        'ffn2_w_down': n(18, (DEPTH, D_FF, D_MODEL), D_FF ** -0.5),
        'w_in': n(19, (DEPTH, D_MODEL, IN_WIDTH), D_MODEL ** -0.5),
        'b_forget': FORGET_BIAS + n(20, (DEPTH, N_HEADS), 0.1),
        'b_gate': n(21, (DEPTH, 2 * D_MODEL), 0.01),
        'ssm_lam_re': -0.5 * jnp.exp(n(22, (DEPTH, N_GROUPS, STATE_DIM), 0.01)),
        'ssm_lam_im': lam_im_base + n(23, (DEPTH, N_GROUPS, STATE_DIM), 0.01),
        'ssm_log_dt': jax.random.uniform(ks[24], (DEPTH, N_GROUPS), F32, math.log(DT_MIN), math.log(DT_MAX)),
        'ssm_b_re': n(25, (DEPTH, N_GROUPS, STATE_DIM, SSM_GROUP), (2 * SSM_GROUP) ** -0.5),
        'ssm_b_im': n(26, (DEPTH, N_GROUPS, STATE_DIM, SSM_GROUP), (2 * SSM_GROUP) ** -0.5),
        'ssm_c_re': n(27, (DEPTH, N_GROUPS, SSM_GROUP, STATE_DIM), (2 * STATE_DIM) ** -0.5),
        'ssm_c_im': n(28, (DEPTH, N_GROUPS, SSM_GROUP, STATE_DIM), (2 * STATE_DIM) ** -0.5),
        'ssm_d': n(29, (DEPTH, SSM_WIDTH), 1.0),
        'ssm_w_glu': n(30, (DEPTH, SSM_WIDTH, SSM_WIDTH), SSM_WIDTH ** -0.5),
        'ssm_b_glu': n(31, (DEPTH, SSM_WIDTH), 0.01),
        'w_proj_attn': n(32, (DEPTH, ATT_WIDTH, D_MODEL), ATT_WIDTH ** -0.5),
        'w_proj_ssm': n(33, (DEPTH, SSM_WIDTH, D_MODEL), SSM_WIDTH ** -0.5),
        'w_out': n(34, (DEPTH, D_MODEL, D_MODEL), D_MODEL ** -0.5),
    }


def reference(x_prompt, x_sample, cache_k, cache_v, cache_logf, state_ssm_re, state_ssm_im,
              norm_f1_pre, norm_f1_post, norm_mix_pre, norm_mix_post, norm_f2_pre, norm_f2_post,
              ffn1_w_gate, ffn1_w_up, ffn1_w_down, ffn2_w_gate, ffn2_w_up, ffn2_w_down,
              w_in, b_forget, b_gate, ssm_lam_re, ssm_lam_im, ssm_log_dt, ssm_b_re, ssm_b_im,
              ssm_c_re, ssm_c_im, ssm_d, ssm_w_glu, ssm_b_glu, w_proj_attn, w_proj_ssm, w_out):
    xp, xs = x_prompt, x_sample
    kp, vp, fp, hrp, hip = [], [], [], [], []
    ksl, vsl, fsl, hrs, his = [], [], [], [], []
    h0_zero = jnp.zeros((x_prompt.shape[0], N_GROUPS, STATE_DIM), F32)
    for l in range(DEPTH):
        lw = dict(norm_f1_pre=norm_f1_pre[l], norm_f1_post=norm_f1_post[l],
                  norm_mix_pre=norm_mix_pre[l], norm_mix_post=norm_mix_post[l],
                  norm_f2_pre=norm_f2_pre[l], norm_f2_post=norm_f2_post[l],
                  ffn1_w_gate=ffn1_w_gate[l], ffn1_w_up=ffn1_w_up[l], ffn1_w_down=ffn1_w_down[l],
                  ffn2_w_gate=ffn2_w_gate[l], ffn2_w_up=ffn2_w_up[l], ffn2_w_down=ffn2_w_down[l],
                  w_in=w_in[l], b_forget=b_forget[l], b_gate=b_gate[l],
                  ssm_lam_re=ssm_lam_re[l], ssm_lam_im=ssm_lam_im[l], ssm_log_dt=ssm_log_dt[l],
                  ssm_b_re=ssm_b_re[l], ssm_b_im=ssm_b_im[l], ssm_c_re=ssm_c_re[l], ssm_c_im=ssm_c_im[l],
                  ssm_d=ssm_d[l], ssm_w_glu=ssm_w_glu[l], ssm_b_glu=ssm_b_glu[l],
                  w_proj_attn=w_proj_attn[l], w_proj_ssm=w_proj_ssm[l], w_out=w_out[l])
        xp, k1, v1, f1, r1, i1 = layer(xp, lw, None, h0_zero, h0_zero)
        kp.append(k1); vp.append(v1); fp.append(f1); hrp.append(r1); hip.append(i1)
        xs, k2, v2, f2, r2, i2 = layer(xs, lw, (cache_k[l], cache_v[l], cache_logf[l]),
                                       state_ssm_re[l], state_ssm_im[l])
        ksl.append(k2); vsl.append(v2); fsl.append(f2); hrs.append(r2); his.append(i2)
    return (xp, xs,
            jnp.stack(kp), jnp.stack(vp), jnp.stack(fp), jnp.stack(hrp), jnp.stack(hip),
            jnp.stack(ksl), jnp.stack(vsl), jnp.stack(fsl), jnp.stack(hrs), jnp.stack(his))
```

```python
import functools
import math

import jax
import jax.numpy as jnp
from jax import lax
from jax.experimental import pallas as pl
from jax.experimental.pallas import tpu as pltpu

F32 = jnp.float32
BF16 = jnp.bfloat16

D_MODEL = 2048
SEQ = 8192
DEPTH = 4
DEC_BATCH = 32
DEC_SEQ = 32
PAST_LEN = 1024
N_HEADS = 16
HEAD_DIM = 64
ATT_WIDTH = N_HEADS * HEAD_DIM
SSM_WIDTH = D_MODEL // 2
SSM_GROUP = 16
N_GROUPS = SSM_WIDTH // SSM_GROUP
STATE_DIM = 64
D_FF = 5504
EPS = 1e-6

N_PROMPT = SEQ
N_SAMPLE = DEC_BATCH * DEC_SEQ
N_TOK = N_PROMPT + N_SAMPLE

LANES = 128
D_FF_PAD = 5632
FFN_TM = 512
FFN_TF = 512
INPROJ_TM = 512
INPROJ_TN = 1024
ATT_TQ = 512
MERGE_TM = 256
SSM_LB = 8
SSM_GB = 8
N_GB = N_GROUPS // SSM_GB
SSM_K = SSM_LB * LANES
SSM_S = SSM_GB * STATE_DIM
NEG_BIG = -1e30
VMEM_LIMIT = 56 * 1024 * 1024


def _cparams(sem):
    return pltpu.CompilerParams(dimension_semantics=sem, vmem_limit_bytes=VMEM_LIMIT)


def _rms(x, g):
    ms = jnp.mean(x * x, axis=-1, keepdims=True)
    return x * lax.rsqrt(ms + EPS) * g


def _ffn_kernel(x_ref, gpre_ref, gpost_ref, wg_ref, wu_ref, wd_ref, o_ref, xn_ref):
    f = pl.program_id(1)
    nf = pl.num_programs(1)

    @pl.when(f == 0)
    def _():
        xn_ref[...] = _rms(x_ref[...], gpre_ref[...]).astype(BF16)

    xn = xn_ref[...]
    g = jnp.dot(xn, wg_ref[...], preferred_element_type=F32)
    u = jnp.dot(xn, wu_ref[...], preferred_element_type=F32)
    h = (g * jax.nn.sigmoid(g) * u).astype(BF16)
    d = jnp.dot(h, wd_ref[...], preferred_element_type=F32)

    @pl.when(f == 0)
    def _():
        o_ref[...] = d

    @pl.when(f > 0)
    def _():
        o_ref[...] += d

    @pl.when(f == nf - 1)
    def _():
        o_ref[...] = x_ref[...] + 0.5 * _rms(o_ref[...], gpost_ref[...])


def _ffn(x, gpre, gpost, wg, wu, wd):
    n = x.shape[0]
    grid = (n // FFN_TM, D_FF_PAD // FFN_TF)
    return pl.pallas_call(
        _ffn_kernel,
        grid=grid,
        in_specs=[
            pl.BlockSpec((FFN_TM, D_MODEL), lambda i, f: (i, 0)),
            pl.BlockSpec((1, D_MODEL), lambda i, f: (0, 0)),
            pl.BlockSpec((1, D_MODEL), lambda i, f: (0, 0)),
            pl.BlockSpec((D_MODEL, FFN_TF), lambda i, f: (0, f)),
            pl.BlockSpec((D_MODEL, FFN_TF), lambda i, f: (0, f)),
            pl.BlockSpec((FFN_TF, D_MODEL), lambda i, f: (f, 0)),
        ],
        out_specs=pl.BlockSpec((FFN_TM, D_MODEL), lambda i, f: (i, 0)),
        out_shape=jax.ShapeDtypeStruct((n, D_MODEL), F32),
        scratch_shapes=[pltpu.VMEM((FFN_TM, D_MODEL), BF16)],
        compiler_params=_cparams(("parallel", "arbitrary")),
        name="ffn",
    )(x, gpre, gpost, wg, wu, wd)


def _log_sigmoid(x):
    return jnp.minimum(x, 0.0) - jnp.log1p(jnp.exp(-jnp.abs(x)))


def _inproj_kernel(x_ref, g_ref, w_ref, wf_ref, bf_ref, bg_ref,
                   q_ref, kf_ref, vf_ref, kb_ref, vb_ref, s_ref, gate_ref, lf_ref, xn_ref):
    j = pl.program_id(1)

    @pl.when(j == 0)
    def _():
        xn = _rms(x_ref[...], g_ref[...]).astype(BF16)
        xn_ref[...] = xn
        fl = jnp.dot(xn, wf_ref[...], preferred_element_type=F32) + bf_ref[...]
        lf_ref[...] = _log_sigmoid(fl)

    z = jnp.dot(xn_ref[...], w_ref[...], preferred_element_type=F32)

    @pl.when(j == 0)
    def _():
        q_ref[...] = (z * (HEAD_DIM ** -0.5)).astype(BF16)

    @pl.when(j == 1)
    def _():
        kf_ref[...] = z
        kb_ref[...] = z.astype(BF16)

    @pl.when(j == 2)
    def _():
        vf_ref[...] = z
        vb_ref[...] = z.astype(BF16)

    @pl.when(j == 3)
    def _():
        s_ref[...] = z

    for jj in range(2 * D_MODEL // INPROJ_TN):
        @pl.when(j == 4 + jj)
        def _(jj=jj):
            sl = slice(jj * INPROJ_TN, (jj + 1) * INPROJ_TN)
            gate_ref[:, sl] = jax.nn.sigmoid(z + bg_ref[:, sl]).astype(BF16)


def _inproj(x, g, w, wf, bf, bg):
    n = x.shape[0]
    tm = INPROJ_TM
    ncol = w.shape[1] // INPROJ_TN
    row = lambda i, j: (i, 0)
    const = lambda i, j: (0, 0)
    outs = (
        jax.ShapeDtypeStruct((n, ATT_WIDTH), BF16),
        jax.ShapeDtypeStruct((n, ATT_WIDTH), F32),
        jax.ShapeDtypeStruct((n, ATT_WIDTH), F32),
        jax.ShapeDtypeStruct((n, ATT_WIDTH), BF16),
        jax.ShapeDtypeStruct((n, ATT_WIDTH), BF16),
        jax.ShapeDtypeStruct((n, SSM_WIDTH), F32),
        jax.ShapeDtypeStruct((n, 2 * D_MODEL), BF16),
        jax.ShapeDtypeStruct((n, LANES), F32),
    )
    return pl.pallas_call(
        _inproj_kernel,
        grid=(n // tm, ncol),
        in_specs=[
            pl.BlockSpec((tm, D_MODEL), row),
            pl.BlockSpec((1, D_MODEL), const),
            pl.BlockSpec((D_MODEL, INPROJ_TN), lambda i, j: (0, j)),
            pl.BlockSpec((D_MODEL, LANES), const),
            pl.BlockSpec((1, LANES), const),
            pl.BlockSpec((1, 2 * D_MODEL), const),
        ],
        out_specs=[pl.BlockSpec((tm, o.shape[1]), row) for o in outs],
        out_shape=outs,
        scratch_shapes=[pltpu.VMEM((tm, D_MODEL), BF16)],
        compiler_params=_cparams(("parallel", "arbitrary")),
        name="inproj",
    )(x, g, w, wf, bf, bg)


def _attn_prompt_kernel(q_ref, k_ref, v_ref, ck_ref, o_ref, m_ref, l_ref, acc_ref):
    i = pl.program_id(1)
    kk = pl.program_id(2)
    tq = q_ref.shape[0]
    tk = k_ref.shape[0]

    @pl.when(kk == 0)
    def _():
        m_ref[...] = jnp.full(m_ref.shape, NEG_BIG, F32)
        l_ref[...] = jnp.zeros(l_ref.shape, F32)
        acc_ref[...] = jnp.zeros(acc_ref.shape, F32)

    def step(masked):
        q = q_ref[...]
        k = k_ref[...]
        v = v_ref[...]
        lane = lax.broadcasted_iota(jnp.int32, q.shape, 1)
        if masked:
            row = lax.broadcasted_iota(jnp.int32, (tq, tk), 0)
            col = lax.broadcasted_iota(jnp.int32, (tq, tk), 1)
            keep = col <= row
        for h in range(2):
            sel = (lane < HEAD_DIM) if h == 0 else (lane >= HEAD_DIM)
            qh = jnp.where(sel, q, jnp.zeros_like(q))
            s = lax.dot_general(qh, k, (((1,), (1,)), ((), ())), preferred_element_type=F32)
            s = s - ck_ref[0, h:h + 1, :]
            if masked:
                s = jnp.where(keep, s, NEG_BIG)
            m_prev = m_ref[h]
            m_new = jnp.maximum(m_prev, jnp.max(s, axis=-1, keepdims=True))
            alpha = jnp.exp(m_prev - m_new)
            p = jnp.exp(s - m_new)
            l_ref[h] = alpha * l_ref[h] + jnp.sum(p, axis=-1, keepdims=True)
            acc_ref[h] = alpha * acc_ref[h] + jnp.dot(p.astype(BF16), v, preferred_element_type=F32)
            m_ref[h] = m_new

    @pl.when(kk < i)
    def _():
        step(False)

    @pl.when(kk == i)
    def _():
        step(True)
        lane = lax.broadcasted_iota(jnp.int32, (tq, LANES), 1)
        o0 = acc_ref[0] / l_ref[0]
        o1 = acc_ref[1] / l_ref[1]
        o_ref[...] = jnp.where(lane < HEAD_DIM, o0, o1).astype(o_ref.dtype)


def _attn_prompt(q, kb, vb, ck):
    t = ATT_TQ
    nq = N_PROMPT // t
    kv_map = lambda j, i, kk: (jnp.minimum(kk, i), j)
    return pl.pallas_call(
        _attn_prompt_kernel,
        grid=(N_HEADS // 2, nq, nq),
        in_specs=[
            pl.BlockSpec((t, LANES), lambda j, i, kk: (i, j)),
            pl.BlockSpec((t, LANES), kv_map),
            pl.BlockSpec((t, LANES), kv_map),
            pl.BlockSpec((1, 2, t), lambda j, i, kk: (j, 0, jnp.minimum(kk, i))),
        ],
        out_specs=pl.BlockSpec((t, LANES), lambda j, i, kk: (i, j)),
        out_shape=jax.ShapeDtypeStruct((N_PROMPT, ATT_WIDTH), BF16),
        scratch_shapes=[
            pltpu.VMEM((2, t, 1), F32),
            pltpu.VMEM((2, t, 1), F32),
            pltpu.VMEM((2, t, LANES), F32),
        ],
        compiler_params=_cparams(("parallel", "parallel", "arbitrary")),
        name="attn_prompt",
    )(q, kb, vb, ck)


def _attn_sample_kernel(q_ref, kn_ref, vn_ref, kc_ref, vc_ref, ckc_ref, ckn_ref, o_ref):
    t = q_ref.shape[0]
    q = q_ref[...]
    lane = lax.broadcasted_iota(jnp.int32, q.shape, 1)
    zero = jnp.zeros_like(q)
    q2 = jnp.concatenate([jnp.where(lane < HEAD_DIM, q, zero),
                          jnp.where(lane >= HEAD_DIM, q, zero)], axis=0)
    kc = kc_ref[0].astype(BF16)
    vc = vc_ref[0].astype(BF16)
    kn = kn_ref[...]
    vn = vn_ref[...]
    dn = (((1,), (1,)), ((), ()))
    s_c = lax.dot_general(q2, kc, dn, preferred_element_type=F32)
    s_n = lax.dot_general(q2, kn, dn, preferred_element_type=F32)
    ckc = ckc_ref[0, 0]
    ckn = ckn_ref[0, 0]
    first = lax.broadcasted_iota(jnp.int32, (2 * t, 1), 0) < t
    s_c = s_c - jnp.where(first, ckc[0:1, :], ckc[1:2, :])
    s_n = s_n - jnp.where(first, ckn[0:1, :], ckn[1:2, :])
    row = lax.broadcasted_iota(jnp.int32, (2 * t, t), 0)
    col = lax.broadcasted_iota(jnp.int32, (2 * t, t), 1)
    qpos = jnp.where(row < t, row, row - t)
    s_n = jnp.where(col <= qpos, s_n, NEG_BIG)
    m = jnp.maximum(jnp.max(s_c, axis=-1, keepdims=True), jnp.max(s_n, axis=-1, keepdims=True))
    p_c = jnp.exp(s_c - m)
    p_n = jnp.exp(s_n - m)
    l = jnp.sum(p_c, axis=-1, keepdims=True) + jnp.sum(p_n, axis=-1, keepdims=True)
    o = (jnp.dot(p_c.astype(BF16), vc, preferred_element_type=F32)
         + jnp.dot(p_n.astype(BF16), vn, preferred_element_type=F32)) / l
    o_ref[...] = jnp.where(lane < HEAD_DIM, o[:t], o[t:]).astype(o_ref.dtype)


def _attn_sample(q, kb, vb, cache_k, cache_v, ckc, ckn):
    t = DEC_SEQ
    tok = lambda b, j: (b, j)
    return pl.pallas_call(
        _attn_sample_kernel,
        grid=(DEC_BATCH, N_HEADS // 2),
        in_specs=[
            pl.BlockSpec((t, LANES), tok),
            pl.BlockSpec((t, LANES), tok),
            pl.BlockSpec((t, LANES), tok),
            pl.BlockSpec((1, PAST_LEN, LANES), lambda b, j: (b, 0, j)),
            pl.BlockSpec((1, PAST_LEN, LANES), lambda b, j: (b, 0, j)),
            pl.BlockSpec((1, 1, 2, PAST_LEN), lambda b, j: (b, j, 0, 0)),
            pl.BlockSpec((1, 1, 2, t), lambda b, j: (b, j, 0, 0)),
        ],
        out_specs=pl.BlockSpec((t, LANES), tok),
        out_shape=jax.ShapeDtypeStruct((N_SAMPLE, ATT_WIDTH), BF16),
        compiler_params=_cparams(("parallel", "parallel")),
        name="attn_sample",
    )(q, kb, vb, cache_k, cache_v, ckc, ckn)


def _ssm_kernel(nseq, nblk, u_ref, wst_ref, win_ref, wca_ref, a_ref, h0_ref, y_ref, hf_ref, sx_ref):
    u = u_ref[0]
    sx_ref[...] = jnp.dot(u, wst_ref[0], preferred_element_type=F32)
    a = a_ref[0]
    are = a[:, :SSM_S]
    aim = a[:, SSM_S:]
    h0 = h0_ref[0]

    def body(j, carry):
        sr, si = carry
        rows = pl.ds(pl.multiple_of(j * nseq, nseq), nseq)
        blk = sx_ref[rows, :]
        sx_ref[rows, :] = jnp.concatenate([sr, si], axis=-1)
        nr = are * sr - aim * si + blk[:, :SSM_S]
        ni = are * si + aim * sr + blk[:, SSM_S:]
        return nr, ni

    sr, si = lax.fori_loop(0, nblk, body, (h0[:, :SSM_S], h0[:, SSM_S:]))
    hf_ref[0] = jnp.concatenate([sr, si], axis=-1)
    y_ref[0] = (jnp.dot(u, win_ref[0], preferred_element_type=F32)
                + jnp.dot(sx_ref[...].astype(BF16), wca_ref[0], preferred_element_type=F32))


def _ssm(u, wst, win, wca, a, h0, nseq, nblk):
    r = nblk * nseq
    slab = lambda g: (g, 0, 0)
    return pl.pallas_call(
        functools.partial(_ssm_kernel, nseq, nblk),
        grid=(N_GB,),
        in_specs=[
            pl.BlockSpec((1, r, SSM_K), slab),
            pl.BlockSpec((1, SSM_K, 2 * SSM_S), slab),
            pl.BlockSpec((1, SSM_K, SSM_K), slab),
            pl.BlockSpec((1, 2 * SSM_S, SSM_K), slab),
            pl.BlockSpec((1, 1, 2 * SSM_S), slab),
            pl.BlockSpec((1, nseq, 2 * SSM_S), slab),
        ],
        out_specs=[
            pl.BlockSpec((1, r, SSM_K), slab),
            pl.BlockSpec((1, nseq, 2 * SSM_S), slab),
        ],
        out_shape=(
            jax.ShapeDtypeStruct((N_GB, r, SSM_K), F32),
            jax.ShapeDtypeStruct((N_GB, nseq, 2 * SSM_S), F32),
        ),
        scratch_shapes=[pltpu.VMEM((r, 2 * SSM_S), F32)],
        compiler_params=_cparams(("parallel",)),
        name="ssm_p" if nseq == 1 else "ssm_s",
    )(u, wst, win, wca, a, h0)


def _ssm_weights(lam_re, lam_im, log_dt, b_re, b_im, c_re, c_im):
    hp = lax.Precision.HIGHEST
    nl = lam_re.shape[0]
    lb = SSM_LB
    dt = jnp.exp(log_dt.astype(F32))[..., None]
    lr, li = lam_re.astype(F32), lam_im.astype(F32)
    mag = jnp.exp(lr * dt)
    a_re = mag * jnp.cos(li * dt)
    a_im = mag * jnp.sin(li * dt)
    nr, ni = a_re - 1.0, a_im
    den = lr * lr + li * li
    f_re = (nr * lr + ni * li) / den
    f_im = (ni * lr - nr * li) / den
    br, bi = b_re.astype(F32), b_im.astype(F32)
    bb_re = f_re[..., None] * br - f_im[..., None] * bi
    bb_im = f_re[..., None] * bi + f_im[..., None] * br
    pr, pi = [jnp.ones_like(a_re)], [jnp.zeros_like(a_re)]
    for _ in range(lb):
        pr, pi = pr + [pr[-1] * a_re - pi[-1] * a_im], pi + [pr[-1] * a_im + pi[-1] * a_re]
    p_re, p_im = jnp.stack(pr), jnp.stack(pi)
    xb_re = p_re[:lb, ..., None] * bb_re - p_im[:lb, ..., None] * bb_im
    xb_im = p_re[:lb, ..., None] * bb_im + p_im[:lb, ..., None] * bb_re
    cr, ci = c_re.astype(F32), c_im.astype(F32)
    kt = (jnp.einsum('lgmp,tlgpn->tlgmn', cr, xb_re, precision=hp)
          - jnp.einsum('lgmp,tlgpn->tlgmn', ci, xb_im, precision=hp))
    eye = jnp.eye(SSM_GB, dtype=F32)

    src = jnp.arange(lb)[:, None]
    dst = jnp.arange(lb)[None, :]
    lag = dst - src
    ktz = jnp.where((lag >= 0)[:, :, None, None, None, None], kt[jnp.clip(lag, 0)], 0.0)
    ktz = ktz.reshape(lb, lb, nl, N_GB, SSM_GB, SSM_GROUP, SSM_GROUP)
    win = jnp.einsum('srlbgmn,gh->lbsgnrhm', ktz, eye).reshape(nl, N_GB, SSM_K, SSM_K)

    xs = jnp.stack([xb_re[::-1], xb_im[::-1]], axis=1)
    xs = xs.reshape(lb, 2, nl, N_GB, SSM_GB, STATE_DIM, SSM_GROUP)
    wst = jnp.einsum('silbgpn,gh->lbsgnihp', xs, eye).reshape(nl, N_GB, SSM_K, 2 * SSM_S)

    m_re = cr[None] * p_re[1:, :, :, None, :] - ci[None] * p_im[1:, :, :, None, :]
    m_im = cr[None] * p_im[1:, :, :, None, :] + ci[None] * p_re[1:, :, :, None, :]
    ms = jnp.stack([m_re, -m_im], axis=0)
    ms = ms.reshape(2, lb, nl, N_GB, SSM_GB, SSM_GROUP, STATE_DIM)
    wca = jnp.einsum('irlbgmp,gh->lbigprhm', ms, eye).reshape(nl, N_GB, 2 * SSM_S, SSM_K)

    a_lb = jnp.concatenate([p_re[lb].reshape(nl, N_GB, 1, SSM_S),
                            p_im[lb].reshape(nl, N_GB, 1, SSM_S)], axis=-1)
    return wst.astype(BF16), win.astype(BF16), wca.astype(BF16), a_lb


def _merge_kernel(x_ref, oatt_ref, y_ref, s_ref, gate_ref, d_ref, wglu_ref, bglu_ref,
                  pa_ref, pb_ref, wo_ref, gpost_ref, o_ref):
    y = jax.nn.gelu(y_ref[...] + d_ref[...] * s_ref[...])
    glu = jnp.dot(y.astype(BF16), wglu_ref[...], preferred_element_type=F32) + bglu_ref[...]
    o_ssm = (y * jax.nn.sigmoid(glu)).astype(BF16)
    pa = jnp.dot(oatt_ref[...], pa_ref[...], preferred_element_type=F32)
    pb = jnp.dot(o_ssm, pb_ref[...], preferred_element_type=F32)
    gate = gate_ref[...].astype(F32)
    merged = gate[:, :D_MODEL] * pa + gate[:, D_MODEL:] * pb
    m = jnp.dot(merged.astype(BF16), wo_ref[...], preferred_element_type=F32)
    o_ref[...] = x_ref[...] + _rms(m, gpost_ref[...])


def _merge(x, o_att, y_raw, s_in, gate, d, wglu, bglu, pa, pb, wo, gpost):
    n = x.shape[0]
    tm = MERGE_TM
    row = lambda i: (i, 0)
    const = lambda i: (0, 0)
    once = pl.Buffered(1)
    return pl.pallas_call(
        _merge_kernel,
        grid=(n // tm,),
        in_specs=[
            pl.BlockSpec((tm, D_MODEL), row),
            pl.BlockSpec((tm, ATT_WIDTH), row),
            pl.BlockSpec((tm, SSM_WIDTH), row),
            pl.BlockSpec((tm, SSM_WIDTH), row),
            pl.BlockSpec((tm, 2 * D_MODEL), row),
            pl.BlockSpec((1, SSM_WIDTH), const),
            pl.BlockSpec((SSM_WIDTH, SSM_WIDTH), const, pipeline_mode=once),
            pl.BlockSpec((1, SSM_WIDTH), const),
            pl.BlockSpec((ATT_WIDTH, D_MODEL), const, pipeline_mode=once),
            pl.BlockSpec((SSM_WIDTH, D_MODEL), const, pipeline_mode=once),
            pl.BlockSpec((D_MODEL, D_MODEL), const, pipeline_mode=once),
            pl.BlockSpec((1, D_MODEL), const),
        ],
        out_specs=pl.BlockSpec((tm, D_MODEL), row),
        out_shape=jax.ShapeDtypeStruct((n, D_MODEL), F32),
        compiler_params=_cparams(("parallel",)),
        name="merge",
    )(x, o_att, y_raw, s_in, gate, d, wglu, bglu, pa, pb, wo, gpost)


def _to_slabs(s, nseq, nblk):
    t = s.reshape(nseq, nblk, SSM_LB, N_GB, LANES)
    return t.transpose(3, 1, 0, 2, 4).reshape(N_GB, nblk * nseq, SSM_K)


def _from_slabs(y, nseq, nblk):
    t = y.reshape(N_GB, nblk, nseq, SSM_LB, LANES)
    return t.transpose(2, 1, 3, 0, 4).reshape(nseq * nblk * SSM_LB, SSM_WIDTH)


def _state_to_slabs(h_re, h_im):
    nseq = h_re.shape[0]
    r = h_re.astype(F32).reshape(nseq, N_GB, SSM_S).transpose(1, 0, 2)
    i = h_im.astype(F32).reshape(nseq, N_GB, SSM_S).transpose(1, 0, 2)
    return jnp.concatenate([r, i], axis=-1)


def _state_from_slabs(h):
    nseq = h.shape[1]
    r = h[:, :, :SSM_S].transpose(1, 0, 2).reshape(nseq, N_GROUPS, STATE_DIM)
    i = h[:, :, SSM_S:].transpose(1, 0, 2).reshape(nseq, N_GROUPS, STATE_DIM)
    return r, i


def _pad_cols(w, n):
    return jnp.pad(w, ((0, 0), (0, 0), (0, n - w.shape[-1])))


def kernel(x_prompt, x_sample, cache_k, cache_v, cache_logf, state_ssm_re, state_ssm_im, norm_f1_pre, norm_f1_post, norm_mix_pre, norm_mix_post, norm_f2_pre, norm_f2_post, ffn1_w_gate, ffn1_w_up, ffn1_w_down, ffn2_w_gate, ffn2_w_up, ffn2_w_down, w_in, b_forget, b_gate, ssm_lam_re, ssm_lam_im, ssm_log_dt, ssm_b_re, ssm_b_im, ssm_c_re, ssm_c_im, ssm_d, ssm_w_glu, ssm_b_glu, w_proj_attn, w_proj_ssm, w_out):
    ffn_w = []
    for wg, wu, wd in ((ffn1_w_gate, ffn1_w_up, ffn1_w_down), (ffn2_w_gate, ffn2_w_up, ffn2_w_down)):
        ffn_w.append((_pad_cols(wg.astype(BF16), D_FF_PAD), _pad_cols(wu.astype(BF16), D_FF_PAD),
                      jnp.pad(wd.astype(BF16), ((0, 0), (0, D_FF_PAD - D_FF), (0, 0)))))
    a3 = 3 * ATT_WIDTH
    w_main = jnp.concatenate([w_in[:, :, :a3], w_in[:, :, a3 + N_HEADS:]], axis=-1).astype(BF16)
    w_f = _pad_cols(w_in[:, :, a3:a3 + N_HEADS].astype(BF16), LANES)
    b_f = _pad_cols(b_forget.astype(F32)[:, None, :], LANES)
    wglu = ssm_w_glu.astype(BF16)
    wpa = w_proj_attn.astype(BF16)
    wpb = w_proj_ssm.astype(BF16)
    wo = w_out.astype(BF16)
    wst, win, wca, a_lb = _ssm_weights(ssm_lam_re, ssm_lam_im, ssm_log_dt, ssm_b_re, ssm_b_im,
                                       ssm_c_re, ssm_c_im)
    row = lambda v: v.astype(F32)[:, None, :]
    g_f1a, g_f1b, g_ma, g_mb, g_f2a, g_f2b = map(row, (norm_f1_pre, norm_f1_post, norm_mix_pre,
                                                       norm_mix_post, norm_f2_pre, norm_f2_post))
    b_g, d_ssm, b_glu = row(b_gate), row(ssm_d), row(ssm_b_glu)

    x = jnp.concatenate([x_prompt.reshape(N_PROMPT, D_MODEL), x_sample.reshape(N_SAMPLE, D_MODEL)], axis=0)
    nblk_p = N_PROMPT // SSM_LB
    nblk_s = DEC_SEQ // SSM_LB
    h0_p = jnp.zeros((N_GB, 1, 2 * SSM_S), F32)
    outs = [[] for _ in range(10)]
    for l in range(DEPTH):
        x = _ffn(x, g_f1a[l], g_f1b[l], *[w[l] for w in ffn_w[0]])
        q, kf, vf, kb, vb, s_in, gate, lf = _inproj(x, g_ma[l], w_main[l], w_f[l], b_f[l], b_g[l])
        logf = lf[:, :N_HEADS]
        logf_p = logf[:N_PROMPT]
        logf_s = logf[N_PROMPT:].reshape(DEC_BATCH, DEC_SEQ, N_HEADS)

        c_p = jnp.cumsum(logf_p, axis=0)
        o_p = _attn_prompt(q, kb, vb, c_p.T.reshape(N_HEADS // 2, 2, N_PROMPT))
        c_s = jnp.cumsum(jnp.concatenate([cache_logf[l].astype(F32), logf_s], axis=1), axis=1)
        c_s = c_s.transpose(0, 2, 1).reshape(DEC_BATCH, N_HEADS // 2, 2, PAST_LEN + DEC_SEQ)
        o_s = _attn_sample(q[N_PROMPT:], kb[N_PROMPT:], vb[N_PROMPT:],
                           cache_k[l].reshape(DEC_BATCH, PAST_LEN, ATT_WIDTH),
                           cache_v[l].reshape(DEC_BATCH, PAST_LEN, ATT_WIDTH),
                           c_s[..., :PAST_LEN], c_s[..., PAST_LEN:])
        o_att = jnp.concatenate([o_p, o_s], axis=0)

        u = s_in.astype(BF16)
        y_p, h_p = _ssm(_to_slabs(u[:N_PROMPT], 1, nblk_p), wst[l], win[l], wca[l], a_lb[l], h0_p, 1, nblk_p)
        h0_s = _state_to_slabs(state_ssm_re[l], state_ssm_im[l])
        y_s, h_s = _ssm(_to_slabs(u[N_PROMPT:], DEC_BATCH, nblk_s), wst[l], win[l], wca[l], a_lb[l],
                        h0_s, DEC_BATCH, nblk_s)
        y_raw = jnp.concatenate([_from_slabs(y_p, 1, nblk_p), _from_slabs(y_s, DEC_BATCH, nblk_s)], axis=0)

        x = _merge(x, o_att, y_raw, s_in, gate, d_ssm[l], wglu[l], b_glu[l], wpa[l], wpb[l], wo[l], g_mb[l])
        x = _ffn(x, g_f2a[l], g_f2b[l], *[w[l] for w in ffn_w[1]])

        hp_re, hp_im = _state_from_slabs(h_p)
        hs_re, hs_im = _state_from_slabs(h_s)
        vals = (kf[:N_PROMPT].reshape(1, SEQ, N_HEADS, HEAD_DIM), vf[:N_PROMPT].reshape(1, SEQ, N_HEADS, HEAD_DIM),
                logf_p.reshape(1, SEQ, N_HEADS), hp_re, hp_im,
                kf[N_PROMPT:].reshape(DEC_BATCH, DEC_SEQ, N_HEADS, HEAD_DIM),
                vf[N_PROMPT:].reshape(DEC_BATCH, DEC_SEQ, N_HEADS, HEAD_DIM),
                logf_s, hs_re, hs_im)
        for o, v in zip(outs, vals):
            o.append(v)
    return (x[:N_PROMPT].reshape(1, SEQ, D_MODEL), x[N_PROMPT:].reshape(DEC_BATCH, DEC_SEQ, D_MODEL),
            *[jnp.stack(o) for o in outs])
```

```python
import functools

import jax
import jax.numpy as jnp
from jax import lax
from jax.experimental import pallas as pl
from jax.experimental.pallas import tpu as pltpu

F32 = jnp.float32
BF16 = jnp.bfloat16

D_MODEL = 2048
SEQ = 8192
DEPTH = 4
DEC_BATCH = 32
DEC_SEQ = 32
PAST_LEN = 1024
N_HEADS = 16
HEAD_DIM = 64
ATT_WIDTH = N_HEADS * HEAD_DIM
SSM_WIDTH = D_MODEL // 2
SSM_GROUP = 16
N_GROUPS = SSM_WIDTH // SSM_GROUP
STATE_DIM = 64
D_FF = 5504
EPS = 1e-6

N_PROMPT = SEQ
N_SAMPLE = DEC_BATCH * DEC_SEQ
N_TOK = N_PROMPT + N_SAMPLE

LANES = 128
D_FF_PAD = 5632
FFN_TM = 512
FFN_TF = 512
INPROJ_TM = 512
INPROJ_TN = 1024
ATT_T = 256
MERGE_TM = 256
SSM_LB = 8
SSM_GB = 8
N_GB = N_GROUPS // SSM_GB
SSM_K = SSM_LB * LANES
SSM_S = SSM_GB * STATE_DIM
NEG_BIG = -1e30
SKIP_GAP = 110.0
NORM_SLACK = 2.0 * 1.01
VMEM_LIMIT = 56 * 1024 * 1024


def _cparams(sem):
    return pltpu.CompilerParams(dimension_semantics=sem, vmem_limit_bytes=VMEM_LIMIT)


def _rms(x, g):
    ms = jnp.mean(x * x, axis=-1, keepdims=True)
    return x * lax.rsqrt(ms + EPS) * g


def _ffn_kernel(x_ref, gpre_ref, gpost_ref, wg_ref, wu_ref, wd_ref, o_ref, xn_ref):
    f = pl.program_id(1)
    nf = pl.num_programs(1)

    @pl.when(f == 0)
    def _():
        xn_ref[...] = _rms(x_ref[...], gpre_ref[...]).astype(BF16)
        o_ref[...] = jnp.zeros(o_ref.shape, F32)

    xn = xn_ref[...]
    g = jnp.dot(xn, wg_ref[...], preferred_element_type=F32)
    u = jnp.dot(xn, wu_ref[...], preferred_element_type=F32)
    h = (g * jax.nn.sigmoid(g) * u).astype(BF16)
    o_ref[...] += jnp.dot(h, wd_ref[...], preferred_element_type=F32)

    @pl.when(f == nf - 1)
    def _():
        o_ref[...] = x_ref[...] + 0.5 * _rms(o_ref[...], gpost_ref[...])


def _ffn(x, gpre, gpost, wg, wu, wd):
    n = x.shape[0]
    grid = (n // FFN_TM, D_FF_PAD // FFN_TF)
    return pl.pallas_call(
        _ffn_kernel,
        grid=grid,
        in_specs=[
            pl.BlockSpec((FFN_TM, D_MODEL), lambda i, f: (i, 0)),
            pl.BlockSpec((1, D_MODEL), lambda i, f: (0, 0)),
            pl.BlockSpec((1, D_MODEL), lambda i, f: (0, 0)),
            pl.BlockSpec((D_MODEL, FFN_TF), lambda i, f: (0, f)),
            pl.BlockSpec((D_MODEL, FFN_TF), lambda i, f: (0, f)),
            pl.BlockSpec((FFN_TF, D_MODEL), lambda i, f: (f, 0)),
        ],
        out_specs=pl.BlockSpec((FFN_TM, D_MODEL), lambda i, f: (i, 0)),
        out_shape=jax.ShapeDtypeStruct((n, D_MODEL), F32),
        scratch_shapes=[pltpu.VMEM((FFN_TM, D_MODEL), BF16)],
        compiler_params=_cparams(("parallel", "arbitrary")),
        name="ffn",
    )(x, gpre, gpost, wg, wu, wd)


def _log_sigmoid(x):
    return jnp.minimum(x, 0.0) - jnp.log1p(jnp.exp(-jnp.abs(x)))


N_PTILES = N_PROMPT // INPROJ_TM


def _inproj_kernel(x_ref, g_ref, w_ref, wf_ref, bf_ref, bg_ref, kp_in, vp_in, ks_in, vs_in,
                   q_ref, s_ref, gate_ref, lf_ref, kp_ref, vp_ref, ks_ref, vs_ref,
                   xn_ref):
    del kp_in, vp_in, ks_in, vs_in
    i = pl.program_id(0)
    j = pl.program_id(1)

    @pl.when(j == 0)
    def _():
        xn = _rms(x_ref[...], g_ref[...]).astype(BF16)
        xn_ref[...] = xn
        fl = jnp.dot(xn, wf_ref[...], preferred_element_type=F32) + bf_ref[...]
        lf_ref[...] = _log_sigmoid(fl)

    z = jnp.dot(xn_ref[...], w_ref[...], preferred_element_type=F32)

    @pl.when(j == 0)
    def _():
        q_ref[...] = (z * (HEAD_DIM ** -0.5)).astype(BF16)

    for jj, (p_ref, s_out) in ((1, (kp_ref, ks_ref)), (2, (vp_ref, vs_ref))):
        @pl.when((j == jj) & (i < N_PTILES))
        def _(p_ref=p_ref):
            p_ref[0] = z

        @pl.when((j == jj) & (i >= N_PTILES))
        def _(s_out=s_out):
            s_out[0] = z

    @pl.when(j == 3)
    def _():
        s_ref[...] = z

    for jj in range(2 * D_MODEL // INPROJ_TN):
        @pl.when(j == 4 + jj)
        def _(jj=jj):
            sl = slice(jj * INPROJ_TN, (jj + 1) * INPROJ_TN)
            gate_ref[:, sl] = jax.nn.sigmoid(z + bg_ref[:, sl]).astype(BF16)


def _inproj(layer, x, g, w, wf, bf, bg, kp, vp, ks, vs):
    n = x.shape[0]
    tm = INPROJ_TM
    ncol = w.shape[1] // INPROJ_TN
    row = lambda i, j: (i, 0)
    const = lambda i, j: (0, 0)
    p_map = lambda i, j: (layer, jnp.minimum(i, N_PTILES - 1), 0)
    s_map = lambda i, j: (layer, jnp.maximum(i - N_PTILES, 0), 0)
    tok_outs = (
        jax.ShapeDtypeStruct((n, ATT_WIDTH), BF16),
        jax.ShapeDtypeStruct((n, SSM_WIDTH), F32),
        jax.ShapeDtypeStruct((n, 2 * D_MODEL), BF16),
        jax.ShapeDtypeStruct((n, LANES), F32),
    )
    kv_outs = tuple(jax.ShapeDtypeStruct(a.shape, a.dtype) for a in (kp, vp, ks, vs))
    any_spec = pl.BlockSpec(memory_space=pl.ANY)
    return pl.pallas_call(
        _inproj_kernel,
        grid=(n // tm, ncol),
        in_specs=[
            pl.BlockSpec((tm, D_MODEL), row),
            pl.BlockSpec((1, D_MODEL), const),
            pl.BlockSpec((D_MODEL, INPROJ_TN), lambda i, j: (0, j)),
            pl.BlockSpec((D_MODEL, LANES), const),
            pl.BlockSpec((1, LANES), const),
            pl.BlockSpec((1, 2 * D_MODEL), const),
            any_spec, any_spec, any_spec, any_spec,
        ],
        out_specs=[pl.BlockSpec((tm, o.shape[1]), row) for o in tok_outs] + [
            pl.BlockSpec((1, tm, ATT_WIDTH), p_map), pl.BlockSpec((1, tm, ATT_WIDTH), p_map),
            pl.BlockSpec((1, tm, ATT_WIDTH), s_map), pl.BlockSpec((1, tm, ATT_WIDTH), s_map),
        ],
        out_shape=tok_outs + kv_outs,
        input_output_aliases={6: 4, 7: 5, 8: 6, 9: 7},
        scratch_shapes=[pltpu.VMEM((tm, D_MODEL), BF16)],
        compiler_params=_cparams(("arbitrary", "arbitrary")),
        name="inproj",
    )(x, g, w, wf, bf, bg, kp, vp, ks, vs)


def _lane_cumsum(x, reverse=False):
    n = x.shape[-1]
    pos = lax.broadcasted_iota(jnp.int32, x.shape, x.ndim - 1)
    sh = 1
    while sh < n:
        if reverse:
            x = x + jnp.where(pos < n - sh, pltpu.roll(x, n - sh, axis=x.ndim - 1), 0.0)
        else:
            x = x + jnp.where(pos >= sh, pltpu.roll(x, sh, axis=x.ndim - 1), 0.0)
        sh *= 2
    return x


def _head_sq_norm_max(x, first):
    sq = x * x
    a = jnp.max(jnp.sum(jnp.where(first, sq, 0.0), axis=-1, keepdims=True), axis=0, keepdims=True)
    b = jnp.max(jnp.sum(jnp.where(first, 0.0, sq), axis=-1, keepdims=True), axis=0, keepdims=True)
    return jnp.concatenate([a, b], axis=0)


def _attn_prompt_kernel(q_ref, kf_ref, vf_ref, lft_ref, o_ref,
                        c_ref, ct_ref, kn_ref, k_ref, va_ref, vb_ref, m_ref, acc_ref):
    i = pl.program_id(1)
    t = ATT_T
    s_len = k_ref.shape[0]
    nt = s_len // t
    lane = lax.broadcasted_iota(jnp.int32, (t, LANES), 1)
    first = lane < HEAD_DIM

    @pl.when(i == 0)
    def _():
        c = _lane_cumsum(lft_ref[0])
        c_ref[...] = c
        for jj in range(nt):
            ct_ref[jj] = c[:, jj * t:(jj + 1) * t]

        def prep(r, kn):
            rows = pl.ds(pl.multiple_of(r * t, t), t)
            kb = kf_ref[0, rows, :].astype(BF16)
            k_ref[rows, :] = kb
            kn = jnp.maximum(kn, _head_sq_norm_max(kb.astype(F32), first))
            v = vf_ref[0, rows, :]
            va_ref[rows, :] = jnp.where(first, v, jnp.where(lane == HEAD_DIM, 1.0, 0.0)).astype(BF16)
            vb_ref[rows, :] = jnp.where(first, jnp.where(lane == 0, 1.0, 0.0), v).astype(BF16)
            return kn

        kn_ref[...] = jnp.sqrt(lax.fori_loop(0, nt, prep, jnp.zeros((2, 1), F32)))

    q = q_ref[...]
    zero = jnp.zeros_like(q)
    qh = (jnp.where(first, q, zero), jnp.where(first, zero, q))

    qn = jnp.sqrt(_head_sq_norm_max(q.astype(F32), first))
    c = c_ref[...]
    pos = lax.broadcasted_iota(jnp.int32, c.shape, 1)
    q0 = i * t
    c_first = jnp.max(jnp.where(pos >= q0, c, -jnp.inf), axis=-1, keepdims=True)
    thr = c_first + NORM_SLACK * qn * kn_ref[...] + SKIP_GAP
    need = jnp.where(pos < q0, jnp.where(c <= thr, 1.0, 0.0), 0.0)
    n_keys = jnp.max(jnp.sum(need, axis=-1, keepdims=True)).astype(jnp.int32)
    n_prev = lax.div(n_keys + (t - 1), t)

    m_ref[...] = jnp.full(m_ref.shape, NEG_BIG, F32)
    acc_ref[...] = jnp.zeros(acc_ref.shape, F32)
    row = lax.broadcasted_iota(jnp.int32, (t, t), 0)
    col = lax.broadcasted_iota(jnp.int32, (t, t), 1)

    def process(j, masked):
        rows = pl.ds(pl.multiple_of(j * t, t), t)
        k = k_ref[rows, :]
        ck = ct_ref[j]
        heads = range(2)
        s = [lax.dot_general(qh[h], k, (((1,), (1,)), ((), ())), preferred_element_type=F32)
             - ck[h:h + 1, :] for h in heads]
        if masked:
            s = [jnp.where(col <= row, s[h], NEG_BIG) for h in heads]
        m_prev = [m_ref[h] for h in heads]
        m_new = [jnp.maximum(m_prev[h], jnp.max(s[h], axis=-1, keepdims=True)) for h in heads]
        p = [jnp.exp(s[h] - jnp.tile(m_new[h], (1, t // LANES))).astype(BF16) for h in heads]
        alpha = [jnp.exp(m_prev[h] - m_new[h]) for h in heads]
        pv = [jnp.dot(p[h], v_ref_h[rows, :], preferred_element_type=F32)
              for h, v_ref_h in enumerate((va_ref, vb_ref))]
        for h in heads:
            acc_ref[h] = alpha[h] * acc_ref[h] + pv[h]
            m_ref[h] = m_new[h]

    process(i, True)

    def back(step, carry):
        process(i - 1 - step, False)
        return carry

    lax.fori_loop(0, n_prev, back, 0)

    a0 = acc_ref[0]
    a1 = acc_ref[1]
    o0 = a0 / a0[:, HEAD_DIM:HEAD_DIM + 1]
    o1 = a1 / a1[:, 0:1]
    o_ref[...] = jnp.where(first, o0, o1).astype(o_ref.dtype)


def _attn_prompt(layer, q, kp, vp, lft):
    t = ATT_T
    nt = N_PROMPT // t
    kv_spec = pl.BlockSpec((1, N_PROMPT, LANES), lambda j, i: (layer, 0, j))
    return pl.pallas_call(
        _attn_prompt_kernel,
        grid=(N_HEADS // 2, nt),
        in_specs=[
            pl.BlockSpec((t, LANES), lambda j, i: (i, j)),
            kv_spec,
            kv_spec,
            pl.BlockSpec((1, 2, N_PROMPT), lambda j, i: (j, 0, 0)),
        ],
        out_specs=pl.BlockSpec((t, LANES), lambda j, i: (i, j)),
        out_shape=jax.ShapeDtypeStruct((N_PROMPT, ATT_WIDTH), BF16),
        scratch_shapes=[
            pltpu.VMEM((2, N_PROMPT), F32),
            pltpu.VMEM((nt, 2, t), F32),
            pltpu.VMEM((2, 1), F32),
            pltpu.VMEM((N_PROMPT, LANES), BF16),
            pltpu.VMEM((N_PROMPT, LANES), BF16),
            pltpu.VMEM((N_PROMPT, LANES), BF16),
            pltpu.VMEM((2, t, LANES), F32),
            pltpu.VMEM((2, t, LANES), F32),
        ],
        compiler_params=_cparams(("parallel", "arbitrary")),
        name="attn_prompt",
    )(q, kp, vp, lft)


def _attn_sample_kernel(q_ref, kn_ref, vn_ref, kc_ref, vc_ref, lfc_ref, lfn_ref, o_ref):
    t = q_ref.shape[0]
    q = q_ref[...]
    lane = lax.broadcasted_iota(jnp.int32, q.shape, 1)
    zero = jnp.zeros_like(q)
    q2 = jnp.concatenate([jnp.where(lane < HEAD_DIM, q, zero),
                          jnp.where(lane >= HEAD_DIM, q, zero)], axis=0)
    kc = kc_ref[0].astype(BF16)
    vc = vc_ref[0].astype(BF16)
    kn = kn_ref[0].astype(BF16)
    vn = vn_ref[0].astype(BF16)
    xc = lfc_ref[0, 0]
    ckc = xc - _lane_cumsum(xc, reverse=True)
    ckn = _lane_cumsum(lfn_ref[0, 0])[:, :t]
    dn = (((1,), (1,)), ((), ()))
    s_c = lax.dot_general(q2, kc, dn, preferred_element_type=F32)
    s_n = lax.dot_general(q2, kn, dn, preferred_element_type=F32)
    first = lax.broadcasted_iota(jnp.int32, (2 * t, 1), 0) < t
    s_c = s_c - jnp.where(first, ckc[0:1, :], ckc[1:2, :])
    s_n = s_n - jnp.where(first, ckn[0:1, :], ckn[1:2, :])
    row = lax.broadcasted_iota(jnp.int32, (2 * t, t), 0)
    col = lax.broadcasted_iota(jnp.int32, (2 * t, t), 1)
    qpos = jnp.where(row < t, row, row - t)
    s_n = jnp.where(col <= qpos, s_n, NEG_BIG)
    m = jnp.maximum(jnp.max(s_c, axis=-1, keepdims=True), jnp.max(s_n, axis=-1, keepdims=True))
    p_c = jnp.exp(s_c - m)
    p_n = jnp.exp(s_n - m)
    l = jnp.sum(p_c, axis=-1, keepdims=True) + jnp.sum(p_n, axis=-1, keepdims=True)
    o = (jnp.dot(p_c.astype(BF16), vc, preferred_element_type=F32)
         + jnp.dot(p_n.astype(BF16), vn, preferred_element_type=F32)) / l
    o_ref[...] = jnp.where(lane < HEAD_DIM, o[:t], o[t:]).astype(o_ref.dtype)


def _attn_sample(layer, q, ks, vs, cache_k, cache_v, lfc, lfn):
    t = DEC_SEQ
    new = lambda b, j: (layer, b, j)
    return pl.pallas_call(
        _attn_sample_kernel,
        grid=(DEC_BATCH, N_HEADS // 2),
        in_specs=[
            pl.BlockSpec((t, LANES), lambda b, j: (N_PROMPT // t + b, j)),
            pl.BlockSpec((1, t, LANES), new),
            pl.BlockSpec((1, t, LANES), new),
            pl.BlockSpec((1, PAST_LEN, LANES), lambda b, j: (b, 0, j)),
            pl.BlockSpec((1, PAST_LEN, LANES), lambda b, j: (b, 0, j)),
            pl.BlockSpec((1, 1, 2, PAST_LEN), lambda b, j: (b, j, 0, 0)),
            pl.BlockSpec((1, 1, 2, LANES), lambda b, j: (b, j, 0, 0)),
        ],
        out_specs=pl.BlockSpec((t, LANES), lambda b, j: (b, j)),
        out_shape=jax.ShapeDtypeStruct((N_SAMPLE, ATT_WIDTH), BF16),
        compiler_params=_cparams(("parallel", "parallel")),
        name="attn_sample",
    )(q, ks, vs, cache_k, cache_v, lfc, lfn)


def _ssm_kernel(nseq, nblk, u_ref, wst_ref, win_ref, wca_ref, a_ref, h0_ref, y_ref, hf_ref, sx_ref):
    u = u_ref[0]
    sx_ref[...] = jnp.dot(u, wst_ref[0], preferred_element_type=F32)
    a = a_ref[0]
    are = a[:, :SSM_S]
    aim = a[:, SSM_S:]
    h0 = h0_ref[0]

    def body(j, carry):
        sr, si = carry
        rows = pl.ds(pl.multiple_of(j * nseq, nseq), nseq)
        blk = sx_ref[rows, :]
        sx_ref[rows, :] = jnp.concatenate([sr, si], axis=-1)
        nr = are * sr - aim * si + blk[:, :SSM_S]
        ni = are * si + aim * sr + blk[:, SSM_S:]
        return nr, ni

    sr, si = lax.fori_loop(0, nblk, body, (h0[:, :SSM_S], h0[:, SSM_S:]))
    hf_ref[0] = jnp.concatenate([sr, si], axis=-1)
    y_ref[0] = (jnp.dot(u, win_ref[0], preferred_element_type=F32)
                + jnp.dot(sx_ref[...].astype(BF16), wca_ref[0], preferred_element_type=F32))


def _ssm(u, wst, win, wca, a, h0, nseq, nblk):
    r = nblk * nseq
    slab = lambda g: (g, 0, 0)
    return pl.pallas_call(
        functools.partial(_ssm_kernel, nseq, nblk),
        grid=(N_GB,),
        in_specs=[
            pl.BlockSpec((1, r, SSM_K), slab),
            pl.BlockSpec((1, SSM_K, 2 * SSM_S), slab),
            pl.BlockSpec((1, SSM_K, SSM_K), slab),
            pl.BlockSpec((1, 2 * SSM_S, SSM_K), slab),
            pl.BlockSpec((1, 1, 2 * SSM_S), slab),
            pl.BlockSpec((1, nseq, 2 * SSM_S), slab),
        ],
        out_specs=[
            pl.BlockSpec((1, r, SSM_K), slab),
            pl.BlockSpec((1, nseq, 2 * SSM_S), slab),
        ],
        out_shape=(
            jax.ShapeDtypeStruct((N_GB, r, SSM_K), F32),
            jax.ShapeDtypeStruct((N_GB, nseq, 2 * SSM_S), F32),
        ),
        scratch_shapes=[pltpu.VMEM((r, 2 * SSM_S), F32)],
        compiler_params=_cparams(("parallel",)),
        name="ssm_p" if nseq == 1 else "ssm_s",
    )(u, wst, win, wca, a, h0)


def _ssm_weights(lam_re, lam_im, log_dt, b_re, b_im, c_re, c_im):
    hp = lax.Precision.HIGHEST
    nl = lam_re.shape[0]
    lb = SSM_LB
    dt = jnp.exp(log_dt.astype(F32))[..., None]
    lr, li = lam_re.astype(F32), lam_im.astype(F32)
    mag = jnp.exp(lr * dt)
    a_re = mag * jnp.cos(li * dt)
    a_im = mag * jnp.sin(li * dt)
    nr, ni = a_re - 1.0, a_im
    den = lr * lr + li * li
    f_re = (nr * lr + ni * li) / den
    f_im = (ni * lr - nr * li) / den
    br, bi = b_re.astype(F32), b_im.astype(F32)
    bb_re = f_re[..., None] * br - f_im[..., None] * bi
    bb_im = f_re[..., None] * bi + f_im[..., None] * br
    pr, pi = [jnp.ones_like(a_re)], [jnp.zeros_like(a_re)]
    for _ in range(lb):
        pr, pi = pr + [pr[-1] * a_re - pi[-1] * a_im], pi + [pr[-1] * a_im + pi[-1] * a_re]
    p_re, p_im = jnp.stack(pr), jnp.stack(pi)
    xb_re = p_re[:lb, ..., None] * bb_re - p_im[:lb, ..., None] * bb_im
    xb_im = p_re[:lb, ..., None] * bb_im + p_im[:lb, ..., None] * bb_re
    cr, ci = c_re.astype(F32), c_im.astype(F32)
    kt = (jnp.einsum('lgmp,tlgpn->tlgmn', cr, xb_re, precision=hp)
          - jnp.einsum('lgmp,tlgpn->tlgmn', ci, xb_im, precision=hp))
    eye = jnp.eye(SSM_GB, dtype=F32)[:, None, None, :, None]

    src = jnp.arange(lb)[:, None]
    dst = jnp.arange(lb)[None, :]
    lag = dst - src
    ktz = jnp.where((lag >= 0)[:, :, None, None, None, None], kt[jnp.clip(lag, 0)], 0.0)
    ktz = ktz.reshape(lb, lb, nl, N_GB, SSM_GB, SSM_GROUP, SSM_GROUP).transpose(2, 3, 0, 4, 6, 1, 5)
    win = (ktz[..., None, :] * eye).astype(BF16).reshape(nl, N_GB, SSM_K, SSM_K)

    xs = jnp.stack([xb_re[::-1], xb_im[::-1]], axis=1)
    xs = xs.reshape(lb, 2, nl, N_GB, SSM_GB, STATE_DIM, SSM_GROUP).transpose(2, 3, 0, 4, 6, 1, 5)
    wst = (xs[..., None, :] * eye).astype(BF16).reshape(nl, N_GB, SSM_K, 2 * SSM_S)

    m_re = cr[None] * p_re[1:, :, :, None, :] - ci[None] * p_im[1:, :, :, None, :]
    m_im = cr[None] * p_im[1:, :, :, None, :] + ci[None] * p_re[1:, :, :, None, :]
    ms = jnp.stack([m_re, -m_im], axis=0)
    ms = ms.reshape(2, lb, nl, N_GB, SSM_GB, SSM_GROUP, STATE_DIM).transpose(2, 3, 0, 4, 6, 1, 5)
    wca = (ms[..., None, :] * eye).astype(BF16).reshape(nl, N_GB, 2 * SSM_S, SSM_K)

    a_lb = jnp.concatenate([p_re[lb].reshape(nl, N_GB, 1, SSM_S),
                            p_im[lb].reshape(nl, N_GB, 1, SSM_S)], axis=-1)
    return wst, win, wca, a_lb


def _merge_kernel(x_ref, oatt_ref, y_ref, s_ref, gate_ref, d_ref, wglu_ref, bglu_ref,
                  pa_ref, pb_ref, wo_ref, gpost_ref, o_ref):
    y = jax.nn.gelu(y_ref[...] + d_ref[...] * s_ref[...])
    glu = jnp.dot(y.astype(BF16), wglu_ref[...], preferred_element_type=F32) + bglu_ref[...]
    o_ssm = (y * jax.nn.sigmoid(glu)).astype(BF16)
    pa = jnp.dot(oatt_ref[...], pa_ref[...], preferred_element_type=F32)
    pb = jnp.dot(o_ssm, pb_ref[...], preferred_element_type=F32)
    gate = gate_ref[...].astype(F32)
    merged = gate[:, :D_MODEL] * pa + gate[:, D_MODEL:] * pb
    m = jnp.dot(merged.astype(BF16), wo_ref[...], preferred_element_type=F32)
    o_ref[...] = x_ref[...] + _rms(m, gpost_ref[...])


def _merge(x, o_att, y_raw, s_in, gate, d, wglu, bglu, pa, pb, wo, gpost):
    n = x.shape[0]
    tm = MERGE_TM
    row = lambda i: (i, 0)
    const = lambda i: (0, 0)
    once = pl.Buffered(1)
    return pl.pallas_call(
        _merge_kernel,
        grid=(n // tm,),
        in_specs=[
            pl.BlockSpec((tm, D_MODEL), row),
            pl.BlockSpec((tm, ATT_WIDTH), row),
            pl.BlockSpec((tm, SSM_WIDTH), row),
            pl.BlockSpec((tm, SSM_WIDTH), row),
            pl.BlockSpec((tm, 2 * D_MODEL), row),
            pl.BlockSpec((1, SSM_WIDTH), const),
            pl.BlockSpec((SSM_WIDTH, SSM_WIDTH), const, pipeline_mode=once),
            pl.BlockSpec((1, SSM_WIDTH), const),
            pl.BlockSpec((ATT_WIDTH, D_MODEL), const, pipeline_mode=once),
            pl.BlockSpec((SSM_WIDTH, D_MODEL), const, pipeline_mode=once),
            pl.BlockSpec((D_MODEL, D_MODEL), const, pipeline_mode=once),
            pl.BlockSpec((1, D_MODEL), const),
        ],
        out_specs=pl.BlockSpec((tm, D_MODEL), row),
        out_shape=jax.ShapeDtypeStruct((n, D_MODEL), F32),
        compiler_params=_cparams(("parallel",)),
        name="merge",
    )(x, o_att, y_raw, s_in, gate, d, wglu, bglu, pa, pb, wo, gpost)


def _to_slabs(s, nseq, nblk):
    t = s.reshape(nseq, nblk, SSM_LB, N_GB, LANES)
    return t.transpose(3, 1, 0, 2, 4).reshape(N_GB, nblk * nseq, SSM_K)


def _from_slabs(y, nseq, nblk):
    t = y.reshape(N_GB, nblk, nseq, SSM_LB, LANES)
    return t.transpose(2, 1, 3, 0, 4).reshape(nseq * nblk * SSM_LB, SSM_WIDTH)


def _state_to_slabs(h_re, h_im):
    nseq = h_re.shape[0]
    r = h_re.astype(F32).reshape(nseq, N_GB, SSM_S).transpose(1, 0, 2)
    i = h_im.astype(F32).reshape(nseq, N_GB, SSM_S).transpose(1, 0, 2)
    return jnp.concatenate([r, i], axis=-1)


def _state_from_slabs(h):
    nseq = h.shape[1]
    r = h[:, :, :SSM_S].transpose(1, 0, 2).reshape(nseq, N_GROUPS, STATE_DIM)
    i = h[:, :, SSM_S:].transpose(1, 0, 2).reshape(nseq, N_GROUPS, STATE_DIM)
    return r, i


def _pad_cols(w, n):
    return jnp.pad(w, ((0, 0), (0, 0), (0, n - w.shape[-1])))


def kernel(x_prompt, x_sample, cache_k, cache_v, cache_logf, state_ssm_re, state_ssm_im, norm_f1_pre, norm_f1_post, norm_mix_pre, norm_mix_post, norm_f2_pre, norm_f2_post, ffn1_w_gate, ffn1_w_up, ffn1_w_down, ffn2_w_gate, ffn2_w_up, ffn2_w_down, w_in, b_forget, b_gate, ssm_lam_re, ssm_lam_im, ssm_log_dt, ssm_b_re, ssm_b_im, ssm_c_re, ssm_c_im, ssm_d, ssm_w_glu, ssm_b_glu, w_proj_attn, w_proj_ssm, w_out):
    ffn_w = []
    for wg, wu, wd in ((ffn1_w_gate, ffn1_w_up, ffn1_w_down), (ffn2_w_gate, ffn2_w_up, ffn2_w_down)):
        ffn_w.append((_pad_cols(wg.astype(BF16), D_FF_PAD), _pad_cols(wu.astype(BF16), D_FF_PAD),
                      jnp.pad(wd.astype(BF16), ((0, 0), (0, D_FF_PAD - D_FF), (0, 0)))))
    a3 = 3 * ATT_WIDTH
    w_main = jnp.concatenate([w_in[:, :, :a3], w_in[:, :, a3 + N_HEADS:]], axis=-1).astype(BF16)
    w_f = _pad_cols(w_in[:, :, a3:a3 + N_HEADS].astype(BF16), LANES)
    b_f = _pad_cols(b_forget.astype(F32)[:, None, :], LANES)
    wglu = ssm_w_glu.astype(BF16)
    wpa = w_proj_attn.astype(BF16)
    wpb = w_proj_ssm.astype(BF16)
    wo = w_out.astype(BF16)
    wst, win, wca, a_lb = _ssm_weights(ssm_lam_re, ssm_lam_im, ssm_log_dt, ssm_b_re, ssm_b_im,
                                       ssm_c_re, ssm_c_im)
    row = lambda v: v.astype(F32)[:, None, :]
    g_f1a, g_f1b, g_ma, g_mb, g_f2a, g_f2b = map(row, (norm_f1_pre, norm_f1_post, norm_mix_pre,
                                                       norm_mix_post, norm_f2_pre, norm_f2_post))
    b_g, d_ssm, b_glu = row(b_gate), row(ssm_d), row(ssm_b_glu)

    x = jnp.concatenate([x_prompt.reshape(N_PROMPT, D_MODEL), x_sample.reshape(N_SAMPLE, D_MODEL)], axis=0)
    nblk_p = N_PROMPT // SSM_LB
    nblk_s = DEC_SEQ // SSM_LB
    h0_p = jnp.zeros((N_GB, 1, 2 * SSM_S), F32)
    kp = jnp.zeros((DEPTH, N_PROMPT, ATT_WIDTH), F32)
    vp = jnp.zeros((DEPTH, N_PROMPT, ATT_WIDTH), F32)
    ks = jnp.zeros((DEPTH, N_SAMPLE, ATT_WIDTH), F32)
    vs = jnp.zeros((DEPTH, N_SAMPLE, ATT_WIDTH), F32)
    outs = [[] for _ in range(6)]
    for l in range(DEPTH):
        x = _ffn(x, g_f1a[l], g_f1b[l], *[w[l] for w in ffn_w[0]])
        q, s_in, gate, lf, kp, vp, ks, vs = _inproj(
            l, x, g_ma[l], w_main[l], w_f[l], b_f[l], b_g[l], kp, vp, ks, vs)
        logf = lf[:, :N_HEADS]
        logf_p = logf[:N_PROMPT]
        logf_s = logf[N_PROMPT:].reshape(DEC_BATCH, DEC_SEQ, N_HEADS)

        o_p = _attn_prompt(l, q, kp, vp, logf_p.T.reshape(N_HEADS // 2, 2, N_PROMPT))
        lfc = cache_logf[l].astype(F32).transpose(0, 2, 1).reshape(DEC_BATCH, N_HEADS // 2, 2, PAST_LEN)
        lfn = jnp.pad(logf_s.transpose(0, 2, 1), ((0, 0), (0, 0), (0, LANES - DEC_SEQ)))
        o_s = _attn_sample(l, q, ks, vs,
                           cache_k[l].reshape(DEC_BATCH, PAST_LEN, ATT_WIDTH),
                           cache_v[l].reshape(DEC_BATCH, PAST_LEN, ATT_WIDTH),
                           lfc, lfn.reshape(DEC_BATCH, N_HEADS // 2, 2, LANES))
        o_att = jnp.concatenate([o_p, o_s], axis=0)

        u = s_in.astype(BF16)
        y_p, h_p = _ssm(_to_slabs(u[:N_PROMPT], 1, nblk_p), wst[l], win[l], wca[l], a_lb[l], h0_p, 1, nblk_p)
        h0_s = _state_to_slabs(state_ssm_re[l], state_ssm_im[l])
        y_s, h_s = _ssm(_to_slabs(u[N_PROMPT:], DEC_BATCH, nblk_s), wst[l], win[l], wca[l], a_lb[l],
                        h0_s, DEC_BATCH, nblk_s)
        y_raw = jnp.concatenate([_from_slabs(y_p, 1, nblk_p), _from_slabs(y_s, DEC_BATCH, nblk_s)], axis=0)

        x = _merge(x, o_att, y_raw, s_in, gate, d_ssm[l], wglu[l], b_glu[l], wpa[l], wpb[l], wo[l], g_mb[l])
        x = _ffn(x, g_f2a[l], g_f2b[l], *[w[l] for w in ffn_w[1]])

        hp_re, hp_im = _state_from_slabs(h_p)
        hs_re, hs_im = _state_from_slabs(h_s)
        for o, v in zip(outs, (logf_p.reshape(1, SEQ, N_HEADS), hp_re, hp_im, logf_s, hs_re, hs_im)):
            o.append(v)
    lf_p, hr_p, hi_p, lf_s, hr_s, hi_s = [jnp.stack(o) for o in outs]
    return (x[:N_PROMPT].reshape(1, SEQ, D_MODEL), x[N_PROMPT:].reshape(DEC_BATCH, DEC_SEQ, D_MODEL),
            kp.reshape(DEPTH, 1, SEQ, N_HEADS, HEAD_DIM), vp.reshape(DEPTH, 1, SEQ, N_HEADS, HEAD_DIM),
            lf_p, hr_p, hi_p,
            ks.reshape(DEPTH, DEC_BATCH, DEC_SEQ, N_HEADS, HEAD_DIM),
            vs.reshape(DEPTH, DEC_BATCH, DEC_SEQ, N_HEADS, HEAD_DIM),
            lf_s, hr_s, hi_s)
```

```python
import functools

import jax
import jax.numpy as jnp
from jax import lax
from jax.experimental import pallas as pl
from jax.experimental.pallas import tpu as pltpu

F32 = jnp.float32
BF16 = jnp.bfloat16

D_MODEL = 2048
SEQ = 8192
DEPTH = 4
DEC_BATCH = 32
DEC_SEQ = 32
PAST_LEN = 1024
N_HEADS = 16
HEAD_DIM = 64
ATT_WIDTH = N_HEADS * HEAD_DIM
SSM_WIDTH = D_MODEL // 2
SSM_GROUP = 16
N_GROUPS = SSM_WIDTH // SSM_GROUP
STATE_DIM = 64
D_FF = 5504
EPS = 1e-6

N_PROMPT = SEQ
N_SAMPLE = DEC_BATCH * DEC_SEQ
N_TOK = N_PROMPT + N_SAMPLE

LANES = 128
D_FF_PAD = 5632
FFN_TM = 512
FFN_TF = 512
INPROJ_TM = 512
INPROJ_TN = 1024
ATT_T = 256
MERGE_TM = 256
SSM_LB = 8
SSM_GB = 8
N_GB = N_GROUPS // SSM_GB
SSM_K = SSM_LB * LANES
SSM_S = SSM_GB * STATE_DIM
NEG_BIG = -1e30
SKIP_GAP = 110.0
NORM_SLACK = 2.0 * 1.01
VMEM_LIMIT = 56 * 1024 * 1024


def _cparams(sem):
    return pltpu.CompilerParams(dimension_semantics=sem, vmem_limit_bytes=VMEM_LIMIT)


def _rms(x, g):
    ms = jnp.mean(x * x, axis=-1, keepdims=True)
    return x * lax.rsqrt(ms + EPS) * g


def _ffn_kernel(x_ref, gpre_ref, gpost_ref, wg_ref, wu_ref, wd_ref, o_ref, xn_ref):
    f = pl.program_id(1)
    nf = pl.num_programs(1)

    @pl.when(f == 0)
    def _():
        xn_ref[...] = _rms(x_ref[...], gpre_ref[...]).astype(BF16)
        o_ref[...] = jnp.zeros(o_ref.shape, F32)

    xn = xn_ref[...]
    g = jnp.dot(xn, wg_ref[...], preferred_element_type=F32)
    u = jnp.dot(xn, wu_ref[...], preferred_element_type=F32)
    h = (g * jax.nn.sigmoid(g) * u).astype(BF16)
    o_ref[...] += jnp.dot(h, wd_ref[...], preferred_element_type=F32)

    @pl.when(f == nf - 1)
    def _():
        o_ref[...] = x_ref[...] + 0.5 * _rms(o_ref[...], gpost_ref[...])


def _ffn(x, gpre, gpost, wg, wu, wd):
    n = x.shape[0]
    grid = (n // FFN_TM, D_FF_PAD // FFN_TF)
    return pl.pallas_call(
        _ffn_kernel,
        grid=grid,
        in_specs=[
            pl.BlockSpec((FFN_TM, D_MODEL), lambda i, f: (i, 0)),
            pl.BlockSpec((1, D_MODEL), lambda i, f: (0, 0)),
            pl.BlockSpec((1, D_MODEL), lambda i, f: (0, 0)),
            pl.BlockSpec((D_MODEL, FFN_TF), lambda i, f: (0, f)),
            pl.BlockSpec((D_MODEL, FFN_TF), lambda i, f: (0, f)),
            pl.BlockSpec((FFN_TF, D_MODEL), lambda i, f: (f, 0)),
        ],
        out_specs=pl.BlockSpec((FFN_TM, D_MODEL), lambda i, f: (i, 0)),
        out_shape=jax.ShapeDtypeStruct((n, D_MODEL), F32),
        scratch_shapes=[pltpu.VMEM((FFN_TM, D_MODEL), BF16)],
        compiler_params=_cparams(("parallel", "arbitrary")),
        name="ffn",
    )(x, gpre, gpost, wg, wu, wd)


def _log_sigmoid(x):
    return jnp.minimum(x, 0.0) - jnp.log1p(jnp.exp(-jnp.abs(x)))


N_PTILES = N_PROMPT // INPROJ_TM


def _inproj_kernel(x_ref, g_ref, w_ref, wf_ref, bf_ref, bg_ref, kp_in, vp_in, ks_in, vs_in,
                   q_ref, s_ref, gate_ref, lf_ref, kp_ref, vp_ref, ks_ref, vs_ref,
                   xn_ref):
    del kp_in, vp_in, ks_in, vs_in
    i = pl.program_id(0)
    j = pl.program_id(1)

    @pl.when(j == 0)
    def _():
        xn = _rms(x_ref[...], g_ref[...]).astype(BF16)
        xn_ref[...] = xn
        fl = jnp.dot(xn, wf_ref[...], preferred_element_type=F32) + bf_ref[...]
        lf_ref[...] = _log_sigmoid(fl)

    z = jnp.dot(xn_ref[...], w_ref[...], preferred_element_type=F32)

    @pl.when(j == 0)
    def _():
        q_ref[...] = (z * (HEAD_DIM ** -0.5)).astype(BF16)

    for jj, (p_ref, s_out) in ((1, (kp_ref, ks_ref)), (2, (vp_ref, vs_ref))):
        @pl.when((j == jj) & (i < N_PTILES))
        def _(p_ref=p_ref):
            p_ref[0] = z

        @pl.when((j == jj) & (i >= N_PTILES))
        def _(s_out=s_out):
            s_out[0] = z

    @pl.when(j == 3)
    def _():
        s_ref[...] = z

    for jj in range(2 * D_MODEL // INPROJ_TN):
        @pl.when(j == 4 + jj)
        def _(jj=jj):
            sl = slice(jj * INPROJ_TN, (jj + 1) * INPROJ_TN)
            gate_ref[:, sl] = jax.nn.sigmoid(z + bg_ref[:, sl]).astype(BF16)


def _inproj(layer, x, g, w, wf, bf, bg, kp, vp, ks, vs):
    n = x.shape[0]
    tm = INPROJ_TM
    ncol = w.shape[1] // INPROJ_TN
    row = lambda i, j: (i, 0)
    const = lambda i, j: (0, 0)
    p_map = lambda i, j: (layer, jnp.minimum(i, N_PTILES - 1), 0)
    s_map = lambda i, j: (layer, jnp.maximum(i - N_PTILES, 0), 0)
    tok_outs = (
        jax.ShapeDtypeStruct((n, ATT_WIDTH), BF16),
        jax.ShapeDtypeStruct((n, SSM_WIDTH), F32),
        jax.ShapeDtypeStruct((n, 2 * D_MODEL), BF16),
        jax.ShapeDtypeStruct((n, LANES), F32),
    )
    kv_outs = tuple(jax.ShapeDtypeStruct(a.shape, a.dtype) for a in (kp, vp, ks, vs))
    any_spec = pl.BlockSpec(memory_space=pl.ANY)
    return pl.pallas_call(
        _inproj_kernel,
        grid=(n // tm, ncol),
        in_specs=[
            pl.BlockSpec((tm, D_MODEL), row),
            pl.BlockSpec((1, D_MODEL), const),
            pl.BlockSpec((D_MODEL, INPROJ_TN), lambda i, j: (0, j)),
            pl.BlockSpec((D_MODEL, LANES), const),
            pl.BlockSpec((1, LANES), const),
            pl.BlockSpec((1, 2 * D_MODEL), const),
            any_spec, any_spec, any_spec, any_spec,
        ],
        out_specs=[pl.BlockSpec((tm, o.shape[1]), row) for o in tok_outs] + [
            pl.BlockSpec((1, tm, ATT_WIDTH), p_map), pl.BlockSpec((1, tm, ATT_WIDTH), p_map),
            pl.BlockSpec((1, tm, ATT_WIDTH), s_map), pl.BlockSpec((1, tm, ATT_WIDTH), s_map),
        ],
        out_shape=tok_outs + kv_outs,
        input_output_aliases={6: 4, 7: 5, 8: 6, 9: 7},
        scratch_shapes=[pltpu.VMEM((tm, D_MODEL), BF16)],
        compiler_params=_cparams(("arbitrary", "arbitrary")),
        name="inproj",
    )(x, g, w, wf, bf, bg, kp, vp, ks, vs)


def _lane_cumsum(x, reverse=False):
    n = x.shape[-1]
    pos = lax.broadcasted_iota(jnp.int32, x.shape, x.ndim - 1)
    sh = 1
    while sh < n:
        if reverse:
            x = x + jnp.where(pos < n - sh, pltpu.roll(x, n - sh, axis=x.ndim - 1), 0.0)
        else:
            x = x + jnp.where(pos >= sh, pltpu.roll(x, sh, axis=x.ndim - 1), 0.0)
        sh *= 2
    return x


def _head_sq_norm_max(x, first):
    sq = x * x
    a = jnp.max(jnp.sum(jnp.where(first, sq, 0.0), axis=-1, keepdims=True), axis=0, keepdims=True)
    b = jnp.max(jnp.sum(jnp.where(first, 0.0, sq), axis=-1, keepdims=True), axis=0, keepdims=True)
    return jnp.concatenate([a, b], axis=0)


def _attn_prompt_kernel(q_ref, kf_ref, vf_ref, lft_ref, o_ref,
                        c_ref, ct_ref, kn_ref, k_ref, va_ref, vb_ref, m_ref, acc_ref):
    i = pl.program_id(1)
    t = ATT_T
    s_len = k_ref.shape[0]
    nt = s_len // t
    lane = lax.broadcasted_iota(jnp.int32, (t, LANES), 1)
    first = lane < HEAD_DIM

    @pl.when(i == 0)
    def _():
        c = _lane_cumsum(lft_ref[0])
        c_ref[...] = c
        for jj in range(nt):
            ct_ref[jj] = c[:, jj * t:(jj + 1) * t]

        def prep(r, kn):
            rows = pl.ds(pl.multiple_of(r * t, t), t)
            kb = kf_ref[0, rows, :].astype(BF16)
            k_ref[rows, :] = kb
            kn = jnp.maximum(kn, _head_sq_norm_max(kb.astype(F32), first))
            v = vf_ref[0, rows, :]
            va_ref[rows, :] = jnp.where(first, v, jnp.where(lane == HEAD_DIM, 1.0, 0.0)).astype(BF16)
            vb_ref[rows, :] = jnp.where(first, jnp.where(lane == 0, 1.0, 0.0), v).astype(BF16)
            return kn

        kn_ref[...] = jnp.sqrt(lax.fori_loop(0, nt, prep, jnp.zeros((2, 1), F32)))

    q = q_ref[...]
    zero = jnp.zeros_like(q)
    qh = (jnp.where(first, q, zero), jnp.where(first, zero, q))

    qn = jnp.sqrt(_head_sq_norm_max(q.astype(F32), first))
    c = c_ref[...]
    pos = lax.broadcasted_iota(jnp.int32, c.shape, 1)
    q0 = i * t
    c_first = jnp.max(jnp.where(pos >= q0, c, -jnp.inf), axis=-1, keepdims=True)
    thr = c_first + NORM_SLACK * qn * kn_ref[...] + SKIP_GAP
    need = jnp.where(pos < q0, jnp.where(c <= thr, 1.0, 0.0), 0.0)
    n_keys = jnp.max(jnp.sum(need, axis=-1, keepdims=True)).astype(jnp.int32)
    n_prev = lax.div(n_keys + (t - 1), t)

    m_ref[...] = jnp.full(m_ref.shape, NEG_BIG, F32)
    acc_ref[...] = jnp.zeros(acc_ref.shape, F32)
    row = lax.broadcasted_iota(jnp.int32, (t, t), 0)
    col = lax.broadcasted_iota(jnp.int32, (t, t), 1)

    def process(j, masked):
        rows = pl.ds(pl.multiple_of(j * t, t), t)
        k = k_ref[rows, :]
        ck = ct_ref[j]
        heads = range(2)
        s = [lax.dot_general(qh[h], k, (((1,), (1,)), ((), ())), preferred_element_type=F32)
             - ck[h:h + 1, :] for h in heads]
        if masked:
            s = [jnp.where(col <= row, s[h], NEG_BIG) for h in heads]
        m_prev = [m_ref[h] for h in heads]
        m_new = [jnp.maximum(m_prev[h], jnp.max(s[h], axis=-1, keepdims=True)) for h in heads]
        p = [jnp.exp(s[h] - jnp.tile(m_new[h], (1, t // LANES))).astype(BF16) for h in heads]
        alpha = [jnp.exp(m_prev[h] - m_new[h]) for h in heads]
        pv = [jnp.dot(p[h], v_ref_h[rows, :], preferred_element_type=F32)
              for h, v_ref_h in enumerate((va_ref, vb_ref))]
        for h in heads:
            acc_ref[h] = alpha[h] * acc_ref[h] + pv[h]
            m_ref[h] = m_new[h]

    process(i, True)

    def back(step, carry):
        process(i - 1 - step, False)
        return carry

    lax.fori_loop(0, n_prev, back, 0)

    a0 = acc_ref[0]
    a1 = acc_ref[1]
    o0 = a0 / a0[:, HEAD_DIM:HEAD_DIM + 1]
    o1 = a1 / a1[:, 0:1]
    o_ref[...] = jnp.where(first, o0, o1).astype(o_ref.dtype)


def _attn_prompt(layer, q, kp, vp, lft):
    t = ATT_T
    nt = N_PROMPT // t
    kv_spec = pl.BlockSpec((1, N_PROMPT, LANES), lambda j, i: (layer, 0, j))
    return pl.pallas_call(
        _attn_prompt_kernel,
        grid=(N_HEADS // 2, nt),
        in_specs=[
            pl.BlockSpec((t, LANES), lambda j, i: (i, j)),
            kv_spec,
            kv_spec,
            pl.BlockSpec((1, 2, N_PROMPT), lambda j, i: (j, 0, 0)),
        ],
        out_specs=pl.BlockSpec((t, LANES), lambda j, i: (i, j)),
        out_shape=jax.ShapeDtypeStruct((N_PROMPT, ATT_WIDTH), BF16),
        scratch_shapes=[
            pltpu.VMEM((2, N_PROMPT), F32),
            pltpu.VMEM((nt, 2, t), F32),
            pltpu.VMEM((2, 1), F32),
            pltpu.VMEM((N_PROMPT, LANES), BF16),
            pltpu.VMEM((N_PROMPT, LANES), BF16),
            pltpu.VMEM((N_PROMPT, LANES), BF16),
            pltpu.VMEM((2, t, LANES), F32),
            pltpu.VMEM((2, t, LANES), F32),
        ],
        compiler_params=_cparams(("parallel", "arbitrary")),
        name="attn_prompt",
    )(q, kp, vp, lft)


SAMPLE_HEAD_GROUP = 4


def _attn_sample_kernel(q_ref, kn_ref, vn_ref, kc_ref, vc_ref, lfc_ref, lfn_ref, o_ref):
    t = q_ref.shape[0]
    q = q_ref[...]
    kn = kn_ref[0].astype(BF16)
    vn = vn_ref[0].astype(BF16)
    xc = lfc_ref[0]
    ckc = xc - _lane_cumsum(xc, reverse=True)
    ckn = _lane_cumsum(lfn_ref[0])[:, :t]
    keep = lax.broadcasted_iota(jnp.int32, (t, t), 1) <= lax.broadcasted_iota(jnp.int32, (t, t), 0)
    dn = (((1,), (1,)), ((), ()))
    outs = []
    for h0 in range(0, N_HEADS, SAMPLE_HEAD_GROUP):
        hs = range(h0, h0 + SAMPLE_HEAD_GROUP)
        sl = [slice(h * HEAD_DIM, (h + 1) * HEAD_DIM) for h in hs]
        kc = [kc_ref[0, 0, pl.ds(h, PAST_LEN, stride=N_HEADS), :].astype(BF16) for h in hs]
        vc = [vc_ref[0, 0, pl.ds(h, PAST_LEN, stride=N_HEADS), :].astype(BF16) for h in hs]
        s_c = [lax.dot_general(q[:, s], k, dn, preferred_element_type=F32) - ckc[h:h + 1, :]
               for h, s, k in zip(hs, sl, kc)]
        s_n = [jnp.where(keep, lax.dot_general(q[:, s], kn[:, s], dn, preferred_element_type=F32)
                         - ckn[h:h + 1, :], NEG_BIG) for h, s in zip(hs, sl)]
        m = [jnp.maximum(jnp.max(a, axis=-1, keepdims=True), jnp.max(b, axis=-1, keepdims=True))
             for a, b in zip(s_c, s_n)]
        p_c = [jnp.exp(a - mm) for a, mm in zip(s_c, m)]
        p_n = [jnp.exp(b - mm) for b, mm in zip(s_n, m)]
        l = [jnp.sum(a, axis=-1, keepdims=True) + jnp.sum(b, axis=-1, keepdims=True)
             for a, b in zip(p_c, p_n)]
        outs += [(jnp.dot(a.astype(BF16), v, preferred_element_type=F32)
                  + jnp.dot(b.astype(BF16), vn[:, s], preferred_element_type=F32)) / ll
                 for a, b, v, s, ll in zip(p_c, p_n, vc, sl, l)]
    o_ref[...] = jnp.concatenate(outs, axis=-1).astype(o_ref.dtype)


def _attn_sample(layer, q, ks, vs, cache_k, cache_v, lfc, lfn):
    t = DEC_SEQ
    new = lambda b: (layer, b, 0)
    cache_spec = pl.BlockSpec((1, 1, PAST_LEN * N_HEADS, HEAD_DIM), lambda b: (layer, b, 0, 0))
    return pl.pallas_call(
        _attn_sample_kernel,
        grid=(DEC_BATCH,),
        in_specs=[
            pl.BlockSpec((t, ATT_WIDTH), lambda b: (N_PROMPT // t + b, 0)),
            pl.BlockSpec((1, t, ATT_WIDTH), new),
            pl.BlockSpec((1, t, ATT_WIDTH), new),
            cache_spec,
            cache_spec,
            pl.BlockSpec((1, N_HEADS, PAST_LEN), lambda b: (b, 0, 0)),
            pl.BlockSpec((1, N_HEADS, LANES), lambda b: (b, 0, 0)),
        ],
        out_specs=pl.BlockSpec((t, ATT_WIDTH), lambda b: (b, 0)),
        out_shape=jax.ShapeDtypeStruct((N_SAMPLE, ATT_WIDTH), BF16),
        compiler_params=_cparams(("parallel",)),
        name="attn_sample",
    )(q, ks, vs, cache_k, cache_v, lfc, lfn)


def _ssm_kernel(nseq, nblk, u_ref, wst_ref, win_ref, wca_ref, a_ref, h0_ref, y_ref, hf_ref, sx_ref):
    u = u_ref[0]
    sx_ref[...] = jnp.dot(u, wst_ref[0], preferred_element_type=F32)
    a = a_ref[0]
    are = a[:, :SSM_S]
    aim = a[:, SSM_S:]
    h0 = h0_ref[0]

    def body(j, carry):
        sr, si = carry
        rows = pl.ds(pl.multiple_of(j * nseq, nseq), nseq)
        blk = sx_ref[rows, :]
        sx_ref[rows, :] = jnp.concatenate([sr, si], axis=-1)
        nr = are * sr - aim * si + blk[:, :SSM_S]
        ni = are * si + aim * sr + blk[:, SSM_S:]
        return nr, ni

    sr, si = lax.fori_loop(0, nblk, body, (h0[:, :SSM_S], h0[:, SSM_S:]))
    hf_ref[0] = jnp.concatenate([sr, si], axis=-1)
    y_ref[0] = (jnp.dot(u, win_ref[0], preferred_element_type=F32)
                + jnp.dot(sx_ref[...].astype(BF16), wca_ref[0], preferred_element_type=F32))


def _ssm(u, wst, win, wca, a, h0, nseq, nblk):
    r = nblk * nseq
    slab = lambda g: (g, 0, 0)
    return pl.pallas_call(
        functools.partial(_ssm_kernel, nseq, nblk),
        grid=(N_GB,),
        in_specs=[
            pl.BlockSpec((1, r, SSM_K), slab),
            pl.BlockSpec((1, SSM_K, 2 * SSM_S), slab),
            pl.BlockSpec((1, SSM_K, SSM_K), slab),
            pl.BlockSpec((1, 2 * SSM_S, SSM_K), slab),
            pl.BlockSpec((1, 1, 2 * SSM_S), slab),
            pl.BlockSpec((1, nseq, 2 * SSM_S), slab),
        ],
        out_specs=[
            pl.BlockSpec((1, r, SSM_K), slab),
            pl.BlockSpec((1, nseq, 2 * SSM_S), slab),
        ],
        out_shape=(
            jax.ShapeDtypeStruct((N_GB, r, SSM_K), F32),
            jax.ShapeDtypeStruct((N_GB, nseq, 2 * SSM_S), F32),
        ),
        scratch_shapes=[pltpu.VMEM((r, 2 * SSM_S), F32)],
        compiler_params=_cparams(("parallel",)),
        name="ssm_p" if nseq == 1 else "ssm_s",
    )(u, wst, win, wca, a, h0)


def _ssm_weights(lam_re, lam_im, log_dt, b_re, b_im, c_re, c_im):
    hp = lax.Precision.HIGHEST
    nl = lam_re.shape[0]
    lb = SSM_LB
    dt = jnp.exp(log_dt.astype(F32))[..., None]
    lr, li = lam_re.astype(F32), lam_im.astype(F32)
    mag = jnp.exp(lr * dt)
    a_re = mag * jnp.cos(li * dt)
    a_im = mag * jnp.sin(li * dt)
    nr, ni = a_re - 1.0, a_im
    den = lr * lr + li * li
    f_re = (nr * lr + ni * li) / den
    f_im = (ni * lr - nr * li) / den
    br, bi = b_re.astype(F32), b_im.astype(F32)
    bb_re = f_re[..., None] * br - f_im[..., None] * bi
    bb_im = f_re[..., None] * bi + f_im[..., None] * br
    pr, pi = [jnp.ones_like(a_re)], [jnp.zeros_like(a_re)]
    for _ in range(lb):
        pr, pi = pr + [pr[-1] * a_re - pi[-1] * a_im], pi + [pr[-1] * a_im + pi[-1] * a_re]
    p_re, p_im = jnp.stack(pr), jnp.stack(pi)
    xb_re = p_re[:lb, ..., None] * bb_re - p_im[:lb, ..., None] * bb_im
    xb_im = p_re[:lb, ..., None] * bb_im + p_im[:lb, ..., None] * bb_re
    cr, ci = c_re.astype(F32), c_im.astype(F32)
    kt = (jnp.einsum('lgmp,tlgpn->tlgmn', cr, xb_re, precision=hp)
          - jnp.einsum('lgmp,tlgpn->tlgmn', ci, xb_im, precision=hp))
    ktc = kt.reshape(lb, nl, N_GB, SSM_GB, SSM_GROUP, SSM_GROUP).transpose(1, 2, 0, 3, 5, 4)
    xs = jnp.stack([xb_re[::-1], xb_im[::-1]], axis=1)
    xsc = xs.reshape(lb, 2, nl, N_GB, SSM_GB, STATE_DIM, SSM_GROUP).transpose(2, 3, 0, 1, 4, 6, 5)
    m_re = cr[None] * p_re[1:, :, :, None, :] - ci[None] * p_im[1:, :, :, None, :]
    m_im = cr[None] * p_im[1:, :, :, None, :] + ci[None] * p_re[1:, :, :, None, :]
    ms = jnp.stack([m_re, -m_im], axis=0)
    mcc = ms.reshape(2, lb, nl, N_GB, SSM_GB, SSM_GROUP, STATE_DIM).transpose(2, 3, 0, 1, 4, 6, 5)
    win, wst, wca = _ssm_expand(ktc, xsc, mcc)
    a_lb = jnp.concatenate([p_re[lb].reshape(nl, N_GB, 1, SSM_S),
                            p_im[lb].reshape(nl, N_GB, 1, SSM_S)], axis=-1)
    return wst, win, wca, a_lb


def _ssm_expand_kernel(kt_ref, xs_ref, mc_ref, win_ref, wst_ref, wca_ref, w_ref, bd_ref):
    lb, gb, grp, sd = SSM_LB, SSM_GB, SSM_GROUP, STATE_DIM
    bd_ref[...] = jnp.zeros(bd_ref.shape, F32)
    for tau in range(lb):
        for g in range(gb):
            bd_ref[tau, g * grp:(g + 1) * grp, g * grp:(g + 1) * grp] = kt_ref[0, 0, tau, g]
    w_ref[...] = jnp.zeros(w_ref.shape, F32)
    for r0 in range(lb):
        for r1 in range(r0, lb):
            w_ref[r0 * LANES:(r0 + 1) * LANES, r1 * LANES:(r1 + 1) * LANES] = bd_ref[r1 - r0]
    win_ref[0, 0] = w_ref[...].astype(BF16)

    w_ref[...] = jnp.zeros(w_ref.shape, F32)
    for r0 in range(lb):
        for ri in range(2):
            for g in range(gb):
                rows = slice(r0 * LANES + g * grp, r0 * LANES + (g + 1) * grp)
                cols = slice(ri * SSM_S + g * sd, ri * SSM_S + (g + 1) * sd)
                w_ref[rows, cols] = xs_ref[0, 0, r0, ri, g]
    wst_ref[0, 0] = w_ref[...].astype(BF16)

    w_ref[...] = jnp.zeros(w_ref.shape, F32)
    for ri in range(2):
        for r1 in range(lb):
            for g in range(gb):
                rows = slice(ri * SSM_S + g * sd, ri * SSM_S + (g + 1) * sd)
                cols = slice(r1 * LANES + g * grp, r1 * LANES + (g + 1) * grp)
                w_ref[rows, cols] = mc_ref[0, 0, ri, r1, g]
    wca_ref[0, 0] = w_ref[...].astype(BF16)


def _ssm_expand(ktc, xsc, mcc):
    nl = ktc.shape[0]
    lb, gb, grp, sd = SSM_LB, SSM_GB, SSM_GROUP, STATE_DIM
    slab = lambda l, b: (l, b, 0, 0)
    out = jax.ShapeDtypeStruct((nl, N_GB, SSM_K, SSM_K), BF16)
    return pl.pallas_call(
        _ssm_expand_kernel,
        grid=(nl, N_GB),
        in_specs=[
            pl.BlockSpec((1, 1, lb, gb, grp, grp), lambda l, b: (l, b, 0, 0, 0, 0)),
            pl.BlockSpec((1, 1, lb, 2, gb, grp, sd), lambda l, b: (l, b, 0, 0, 0, 0, 0)),
            pl.BlockSpec((1, 1, 2, lb, gb, sd, grp), lambda l, b: (l, b, 0, 0, 0, 0, 0)),
        ],
        out_specs=[pl.BlockSpec((1, 1, SSM_K, SSM_K), slab)] * 3,
        out_shape=(out, out, out),
        scratch_shapes=[pltpu.VMEM((SSM_K, SSM_K), F32), pltpu.VMEM((lb, LANES, LANES), F32)],
        compiler_params=_cparams(("parallel", "parallel")),
        name="ssm_expand",
    )(ktc, xsc, mcc)


def _merge_kernel(x_ref, oatt_ref, y_ref, s_ref, gate_ref, d_ref, wglu_ref, bglu_ref,
                  pa_ref, pb_ref, wo_ref, gpost_ref, o_ref):
    y = jax.nn.gelu(y_ref[...] + d_ref[...] * s_ref[...])
    glu = jnp.dot(y.astype(BF16), wglu_ref[...], preferred_element_type=F32) + bglu_ref[...]
    o_ssm = (y * jax.nn.sigmoid(glu)).astype(BF16)
    pa = jnp.dot(oatt_ref[...], pa_ref[...], preferred_element_type=F32)
    pb = jnp.dot(o_ssm, pb_ref[...], preferred_element_type=F32)
    gate = gate_ref[...].astype(F32)
    merged = gate[:, :D_MODEL] * pa + gate[:, D_MODEL:] * pb
    m = jnp.dot(merged.astype(BF16), wo_ref[...], preferred_element_type=F32)
    o_ref[...] = x_ref[...] + _rms(m, gpost_ref[...])


def _merge(x, o_att, y_raw, s_in, gate, d, wglu, bglu, pa, pb, wo, gpost):
    n = x.shape[0]
    tm = MERGE_TM
    row = lambda i: (i, 0)
    const = lambda i: (0, 0)
    once = pl.Buffered(1)
    return pl.pallas_call(
        _merge_kernel,
        grid=(n // tm,),
        in_specs=[
            pl.BlockSpec((tm, D_MODEL), row),
            pl.BlockSpec((tm, ATT_WIDTH), row),
            pl.BlockSpec((tm, SSM_WIDTH), row),
            pl.BlockSpec((tm, SSM_WIDTH), row),
            pl.BlockSpec((tm, 2 * D_MODEL), row),
            pl.BlockSpec((1, SSM_WIDTH), const),
            pl.BlockSpec((SSM_WIDTH, SSM_WIDTH), const, pipeline_mode=once),
            pl.BlockSpec((1, SSM_WIDTH), const),
            pl.BlockSpec((ATT_WIDTH, D_MODEL), const, pipeline_mode=once),
            pl.BlockSpec((SSM_WIDTH, D_MODEL), const, pipeline_mode=once),
            pl.BlockSpec((D_MODEL, D_MODEL), const, pipeline_mode=once),
            pl.BlockSpec((1, D_MODEL), const),
        ],
        out_specs=pl.BlockSpec((tm, D_MODEL), row),
        out_shape=jax.ShapeDtypeStruct((n, D_MODEL), F32),
        compiler_params=_cparams(("parallel",)),
        name="merge",
    )(x, o_att, y_raw, s_in, gate, d, wglu, bglu, pa, pb, wo, gpost)


def _to_slabs(s, nseq, nblk):
    t = s.reshape(nseq, nblk, SSM_LB, N_GB, LANES)
    return t.transpose(3, 1, 0, 2, 4).reshape(N_GB, nblk * nseq, SSM_K)


def _from_slabs(y, nseq, nblk):
    t = y.reshape(N_GB, nblk, nseq, SSM_LB, LANES)
    return t.transpose(2, 1, 3, 0, 4).reshape(nseq * nblk * SSM_LB, SSM_WIDTH)


def _state_to_slabs(h_re, h_im):
    nseq = h_re.shape[0]
    r = h_re.astype(F32).reshape(nseq, N_GB, SSM_S).transpose(1, 0, 2)
    i = h_im.astype(F32).reshape(nseq, N_GB, SSM_S).transpose(1, 0, 2)
    return jnp.concatenate([r, i], axis=-1)


def _state_from_slabs(h):
    nseq = h.shape[1]
    r = h[:, :, :SSM_S].transpose(1, 0, 2).reshape(nseq, N_GROUPS, STATE_DIM)
    i = h[:, :, SSM_S:].transpose(1, 0, 2).reshape(nseq, N_GROUPS, STATE_DIM)
    return r, i


def _pad_cols(w, n):
    return jnp.pad(w, ((0, 0), (0, 0), (0, n - w.shape[-1])))


def kernel(x_prompt, x_sample, cache_k, cache_v, cache_logf, state_ssm_re, state_ssm_im, norm_f1_pre, norm_f1_post, norm_mix_pre, norm_mix_post, norm_f2_pre, norm_f2_post, ffn1_w_gate, ffn1_w_up, ffn1_w_down, ffn2_w_gate, ffn2_w_up, ffn2_w_down, w_in, b_forget, b_gate, ssm_lam_re, ssm_lam_im, ssm_log_dt, ssm_b_re, ssm_b_im, ssm_c_re, ssm_c_im, ssm_d, ssm_w_glu, ssm_b_glu, w_proj_attn, w_proj_ssm, w_out):
    ffn_w = []
    for wg, wu, wd in ((ffn1_w_gate, ffn1_w_up, ffn1_w_down), (ffn2_w_gate, ffn2_w_up, ffn2_w_down)):
        ffn_w.append((_pad_cols(wg.astype(BF16), D_FF_PAD), _pad_cols(wu.astype(BF16), D_FF_PAD),
                      jnp.pad(wd.astype(BF16), ((0, 0), (0, D_FF_PAD - D_FF), (0, 0)))))
    a3 = 3 * ATT_WIDTH
    w_main = jnp.concatenate([w_in[:, :, :a3], w_in[:, :, a3 + N_HEADS:]], axis=-1).astype(BF16)
    w_f = _pad_cols(w_in[:, :, a3:a3 + N_HEADS].astype(BF16), LANES)
    b_f = _pad_cols(b_forget.astype(F32)[:, None, :], LANES)
    wglu = ssm_w_glu.astype(BF16)
    wpa = w_proj_attn.astype(BF16)
    wpb = w_proj_ssm.astype(BF16)
    wo = w_out.astype(BF16)
    wst, win, wca, a_lb = _ssm_weights(ssm_lam_re, ssm_lam_im, ssm_log_dt, ssm_b_re, ssm_b_im,
                                       ssm_c_re, ssm_c_im)
    row = lambda v: v.astype(F32)[:, None, :]
    g_f1a, g_f1b, g_ma, g_mb, g_f2a, g_f2b = map(row, (norm_f1_pre, norm_f1_post, norm_mix_pre,
                                                       norm_mix_post, norm_f2_pre, norm_f2_post))
    b_g, d_ssm, b_glu = row(b_gate), row(ssm_d), row(ssm_b_glu)

    x = jnp.concatenate([x_prompt.reshape(N_PROMPT, D_MODEL), x_sample.reshape(N_SAMPLE, D_MODEL)], axis=0)
    nblk_p = N_PROMPT // SSM_LB
    nblk_s = DEC_SEQ // SSM_LB
    h0_p = jnp.zeros((N_GB, 1, 2 * SSM_S), F32)
    lfc_all = cache_logf.astype(F32).transpose(0, 1, 3, 2)
    cache_k2 = cache_k.reshape(DEPTH, DEC_BATCH, PAST_LEN * N_HEADS, HEAD_DIM)
    cache_v2 = cache_v.reshape(DEPTH, DEC_BATCH, PAST_LEN * N_HEADS, HEAD_DIM)
    kp = jnp.zeros((DEPTH, N_PROMPT, ATT_WIDTH), F32)
    vp = jnp.zeros((DEPTH, N_PROMPT, ATT_WIDTH), F32)
    ks = jnp.zeros((DEPTH, N_SAMPLE, ATT_WIDTH), F32)
    vs = jnp.zeros((DEPTH, N_SAMPLE, ATT_WIDTH), F32)
    outs = [[] for _ in range(6)]
    for l in range(DEPTH):
        x = _ffn(x, g_f1a[l], g_f1b[l], *[w[l] for w in ffn_w[0]])
        q, s_in, gate, lf, kp, vp, ks, vs = _inproj(
            l, x, g_ma[l], w_main[l], w_f[l], b_f[l], b_g[l], kp, vp, ks, vs)
        logf = lf[:, :N_HEADS]
        logf_p = logf[:N_PROMPT]
        logf_s = logf[N_PROMPT:].reshape(DEC_BATCH, DEC_SEQ, N_HEADS)

        o_p = _attn_prompt(l, q, kp, vp, logf_p.T.reshape(N_HEADS // 2, 2, N_PROMPT))
        lfn = jnp.pad(logf_s.transpose(0, 2, 1), ((0, 0), (0, 0), (0, LANES - DEC_SEQ)))
        o_s = _attn_sample(l, q, ks, vs, cache_k2, cache_v2, lfc_all[l], lfn)
        o_att = jnp.concatenate([o_p, o_s], axis=0)

        u = s_in.astype(BF16)
        y_p, h_p = _ssm(_to_slabs(u[:N_PROMPT], 1, nblk_p), wst[l], win[l], wca[l], a_lb[l], h0_p, 1, nblk_p)
        h0_s = _state_to_slabs(state_ssm_re[l], state_ssm_im[l])
        y_s, h_s = _ssm(_to_slabs(u[N_PROMPT:], DEC_BATCH, nblk_s), wst[l], win[l], wca[l], a_lb[l],
                        h0_s, DEC_BATCH, nblk_s)
        y_raw = jnp.concatenate([_from_slabs(y_p, 1, nblk_p), _from_slabs(y_s, DEC_BATCH, nblk_s)], axis=0)

        x = _merge(x, o_att, y_raw, s_in, gate, d_ssm[l], wglu[l], b_glu[l], wpa[l], wpb[l], wo[l], g_mb[l])
        x = _ffn(x, g_f2a[l], g_f2b[l], *[w[l] for w in ffn_w[1]])

        hp_re, hp_im = _state_from_slabs(h_p)
        hs_re, hs_im = _state_from_slabs(h_s)
        for o, v in zip(outs, (logf_p.reshape(1, SEQ, N_HEADS), hp_re, hp_im, logf_s, hs_re, hs_im)):
            o.append(v)
    lf_p, hr_p, hi_p, lf_s, hr_s, hi_s = [jnp.stack(o) for o in outs]
    return (x[:N_PROMPT].reshape(1, SEQ, D_MODEL), x[N_PROMPT:].reshape(DEC_BATCH, DEC_SEQ, D_MODEL),
            kp.reshape(DEPTH, 1, SEQ, N_HEADS, HEAD_DIM), vp.reshape(DEPTH, 1, SEQ, N_HEADS, HEAD_DIM),
            lf_p, hr_p, hi_p,
            ks.reshape(DEPTH, DEC_BATCH, DEC_SEQ, N_HEADS, HEAD_DIM),
            vs.reshape(DEPTH, DEC_BATCH, DEC_SEQ, N_HEADS, HEAD_DIM),
            lf_s, hr_s, hi_s)
```

```python
import functools

import jax
import jax.numpy as jnp
from jax import lax
from jax.experimental import pallas as pl
from jax.experimental.pallas import tpu as pltpu

F32 = jnp.float32
BF16 = jnp.bfloat16

D_MODEL = 2048
SEQ = 8192
DEPTH = 4
DEC_BATCH = 32
DEC_SEQ = 32
PAST_LEN = 1024
N_HEADS = 16
HEAD_DIM = 64
ATT_WIDTH = N_HEADS * HEAD_DIM
SSM_WIDTH = D_MODEL // 2
SSM_GROUP = 16
N_GROUPS = SSM_WIDTH // SSM_GROUP
STATE_DIM = 64
D_FF = 5504
EPS = 1e-6

N_PROMPT = SEQ
N_SAMPLE = DEC_BATCH * DEC_SEQ
N_TOK = N_PROMPT + N_SAMPLE

LANES = 128
D_FF_PAD = 5632
FFN_TM = 512
FFN_TF = 512
INPROJ_TM = 512
INPROJ_TN = 1024
ATT_T = 256
MERGE_TM = 256
SSM_LB = 8
SSM_GB = 8
N_GB = N_GROUPS // SSM_GB
SSM_K = SSM_LB * LANES
SSM_S = SSM_GB * STATE_DIM
SSM_SEG = 8
NEG_BIG = -1e30
SKIP_GAP = 110.0
NORM_SLACK = 2.0 * 1.01
VMEM_LIMIT = 56 * 1024 * 1024


def _cparams(sem):
    return pltpu.CompilerParams(dimension_semantics=sem, vmem_limit_bytes=VMEM_LIMIT)


def _rms(x, g):
    ms = jnp.mean(x * x, axis=-1, keepdims=True)
    return x * lax.rsqrt(ms + EPS) * g


def _ffn_kernel(x_ref, gpre_ref, gpost_ref, wg_ref, wu_ref, wd_ref, o_ref, xn_ref):
    f = pl.program_id(1)
    nf = pl.num_programs(1)

    @pl.when(f == 0)
    def _():
        xn_ref[...] = _rms(x_ref[...], gpre_ref[...]).astype(BF16)
        o_ref[...] = jnp.zeros(o_ref.shape, F32)

    xn = xn_ref[...]
    g = jnp.dot(xn, wg_ref[...], preferred_element_type=F32)
    u = jnp.dot(xn, wu_ref[...], preferred_element_type=F32)
    h = (g * jax.nn.sigmoid(g) * u).astype(BF16)
    o_ref[...] += jnp.dot(h, wd_ref[...], preferred_element_type=F32)

    @pl.when(f == nf - 1)
    def _():
        o_ref[...] = x_ref[...] + 0.5 * _rms(o_ref[...], gpost_ref[...])


def _ffn(x, gpre, gpost, wg, wu, wd):
    n = x.shape[0]
    grid = (n // FFN_TM, D_FF_PAD // FFN_TF)
    return pl.pallas_call(
        _ffn_kernel,
        grid=grid,
        in_specs=[
            pl.BlockSpec((FFN_TM, D_MODEL), lambda i, f: (i, 0)),
            pl.BlockSpec((1, D_MODEL), lambda i, f: (0, 0)),
            pl.BlockSpec((1, D_MODEL), lambda i, f: (0, 0)),
            pl.BlockSpec((D_MODEL, FFN_TF), lambda i, f: (0, f)),
            pl.BlockSpec((D_MODEL, FFN_TF), lambda i, f: (0, f)),
            pl.BlockSpec((FFN_TF, D_MODEL), lambda i, f: (f, 0)),
        ],
        out_specs=pl.BlockSpec((FFN_TM, D_MODEL), lambda i, f: (i, 0)),
        out_shape=jax.ShapeDtypeStruct((n, D_MODEL), F32),
        scratch_shapes=[pltpu.VMEM((FFN_TM, D_MODEL), BF16)],
        compiler_params=_cparams(("parallel", "arbitrary")),
        name="ffn",
    )(x, gpre, gpost, wg, wu, wd)


def _log_sigmoid(x):
    return jnp.minimum(x, 0.0) - jnp.log1p(jnp.exp(-jnp.abs(x)))


N_PTILES = N_PROMPT // INPROJ_TM


def _inproj_kernel(x_ref, g_ref, w_ref, wf_ref, bf_ref, bg_ref, kp_in, vp_in, ks_in, vs_in,
                   q_ref, s_ref, gate_ref, lf_ref, kp_ref, vp_ref, ks_ref, vs_ref,
                   xn_ref):
    del kp_in, vp_in, ks_in, vs_in
    i = pl.program_id(0)
    j = pl.program_id(1)

    @pl.when(j == 0)
    def _():
        xn = _rms(x_ref[...], g_ref[...]).astype(BF16)
        xn_ref[...] = xn
        fl = jnp.dot(xn, wf_ref[...], preferred_element_type=F32) + bf_ref[...]
        lf_ref[...] = _log_sigmoid(fl)

    z = jnp.dot(xn_ref[...], w_ref[...], preferred_element_type=F32)

    @pl.when(j == 0)
    def _():
        q_ref[...] = (z * (HEAD_DIM ** -0.5)).astype(BF16)

    for jj, (p_ref, s_out) in ((1, (kp_ref, ks_ref)), (2, (vp_ref, vs_ref))):
        @pl.when((j == jj) & (i < N_PTILES))
        def _(p_ref=p_ref):
            p_ref[0] = z

        @pl.when((j == jj) & (i >= N_PTILES))
        def _(s_out=s_out):
            s_out[0] = z

    @pl.when(j == 3)
    def _():
        s_ref[...] = z

    for jj in range(2 * D_MODEL // INPROJ_TN):
        @pl.when(j == 4 + jj)
        def _(jj=jj):
            sl = slice(jj * INPROJ_TN, (jj + 1) * INPROJ_TN)
            gate_ref[:, sl] = jax.nn.sigmoid(z + bg_ref[:, sl]).astype(BF16)


def _inproj(layer, x, g, w, wf, bf, bg, kp, vp, ks, vs):
    n = x.shape[0]
    tm = INPROJ_TM
    ncol = w.shape[1] // INPROJ_TN
    row = lambda i, j: (i, 0)
    const = lambda i, j: (0, 0)
    p_map = lambda i, j: (layer, jnp.minimum(i, N_PTILES - 1), 0)
    s_map = lambda i, j: (layer, jnp.maximum(i - N_PTILES, 0), 0)
    tok_outs = (
        jax.ShapeDtypeStruct((n, ATT_WIDTH), BF16),
        jax.ShapeDtypeStruct((n, SSM_WIDTH), F32),
        jax.ShapeDtypeStruct((n, 2 * D_MODEL), BF16),
        jax.ShapeDtypeStruct((n, LANES), F32),
    )
    kv_outs = tuple(jax.ShapeDtypeStruct(a.shape, a.dtype) for a in (kp, vp, ks, vs))
    any_spec = pl.BlockSpec(memory_space=pl.ANY)
    return pl.pallas_call(
        _inproj_kernel,
        grid=(n // tm, ncol),
        in_specs=[
            pl.BlockSpec((tm, D_MODEL), row),
            pl.BlockSpec((1, D_MODEL), const),
            pl.BlockSpec((D_MODEL, INPROJ_TN), lambda i, j: (0, j)),
            pl.BlockSpec((D_MODEL, LANES), const),
            pl.BlockSpec((1, LANES), const),
            pl.BlockSpec((1, 2 * D_MODEL), const),
            any_spec, any_spec, any_spec, any_spec,
        ],
        out_specs=[pl.BlockSpec((tm, o.shape[1]), row) for o in tok_outs] + [
            pl.BlockSpec((1, tm, ATT_WIDTH), p_map), pl.BlockSpec((1, tm, ATT_WIDTH), p_map),
            pl.BlockSpec((1, tm, ATT_WIDTH), s_map), pl.BlockSpec((1, tm, ATT_WIDTH), s_map),
        ],
        out_shape=tok_outs + kv_outs,
        input_output_aliases={6: 4, 7: 5, 8: 6, 9: 7},
        scratch_shapes=[pltpu.VMEM((tm, D_MODEL), BF16)],
        compiler_params=_cparams(("arbitrary", "arbitrary")),
        name="inproj",
    )(x, g, w, wf, bf, bg, kp, vp, ks, vs)


def _lane_cumsum(x, reverse=False):
    n = x.shape[-1]
    pos = lax.broadcasted_iota(jnp.int32, x.shape, x.ndim - 1)
    sh = 1
    while sh < n:
        if reverse:
            x = x + jnp.where(pos < n - sh, pltpu.roll(x, n - sh, axis=x.ndim - 1), 0.0)
        else:
            x = x + jnp.where(pos >= sh, pltpu.roll(x, sh, axis=x.ndim - 1), 0.0)
        sh *= 2
    return x


def _head_sq_norm_max(x, first):
    sq = x * x
    a = jnp.max(jnp.sum(jnp.where(first, sq, 0.0), axis=-1, keepdims=True), axis=0, keepdims=True)
    b = jnp.max(jnp.sum(jnp.where(first, 0.0, sq), axis=-1, keepdims=True), axis=0, keepdims=True)
    return jnp.concatenate([a, b], axis=0)


def _attn_prompt_kernel(q_ref, kf_ref, vf_ref, lft_ref, o_ref,
                        c_ref, ct_ref, kn_ref, k_ref, va_ref, vb_ref, m_ref, acc_ref, s_ref):
    i = pl.program_id(1)
    t = ATT_T
    s_len = k_ref.shape[0]
    nt = s_len // t
    lane = lax.broadcasted_iota(jnp.int32, (t, LANES), 1)
    first = lane < HEAD_DIM

    @pl.when(i == 0)
    def _():
        c = _lane_cumsum(lft_ref[0])
        c_ref[...] = c
        for jj in range(nt):
            ct_ref[jj] = c[:, jj * t:(jj + 1) * t]

        def prep(r, kn):
            rows = pl.ds(pl.multiple_of(r * t, t), t)
            kb = kf_ref[0, rows, :].astype(BF16)
            k_ref[rows, :] = kb
            kn = jnp.maximum(kn, _head_sq_norm_max(kb.astype(F32), first))
            v = vf_ref[0, rows, :]
            va_ref[rows, :] = jnp.where(first, v, jnp.where(lane == HEAD_DIM, 1.0, 0.0)).astype(BF16)
            vb_ref[rows, :] = jnp.where(first, jnp.where(lane == 0, 1.0, 0.0), v).astype(BF16)
            return kn

        kn_ref[...] = jnp.sqrt(lax.fori_loop(0, nt, prep, jnp.zeros((2, 1), F32)))

    q = q_ref[...]
    zero = jnp.zeros_like(q)
    qh = (jnp.where(first, q, zero), jnp.where(first, zero, q))

    qn = jnp.sqrt(_head_sq_norm_max(q.astype(F32), first))
    c = c_ref[...]
    pos = lax.broadcasted_iota(jnp.int32, c.shape, 1)
    q0 = i * t
    c_first = jnp.max(jnp.where(pos >= q0, c, -jnp.inf), axis=-1, keepdims=True)
    thr = c_first + NORM_SLACK * qn * kn_ref[...] + SKIP_GAP
    need = jnp.where(pos < q0, jnp.where(c <= thr, 1.0, 0.0), 0.0)
    n_keys = jnp.max(jnp.sum(need, axis=-1, keepdims=True)).astype(jnp.int32)
    n_prev = lax.div(n_keys + (t - 1), t)

    m_ref[...] = jnp.full(m_ref.shape, NEG_BIG, F32)
    acc_ref[...] = jnp.zeros(acc_ref.shape, F32)
    row = lax.broadcasted_iota(jnp.int32, (t, t), 0)
    col = lax.broadcasted_iota(jnp.int32, (t, t), 1)

    heads = range(2)

    def scores(j, masked):
        k = k_ref[pl.ds(pl.multiple_of(j * t, t), t), :]
        ck = ct_ref[j]
        s = [lax.dot_general(qh[h], k, (((1,), (1,)), ((), ())), preferred_element_type=F32)
             - ck[h:h + 1, :] for h in heads]
        if masked:
            s = [jnp.where(col <= row, s[h], NEG_BIG) for h in heads]
        return s

    def accumulate(j, s):
        rows = pl.ds(pl.multiple_of(j * t, t), t)
        m_prev = [m_ref[h] for h in heads]
        m_new = [jnp.maximum(m_prev[h], jnp.max(s[h], axis=-1, keepdims=True)) for h in heads]
        p = [jnp.exp(s[h] - jnp.tile(m_new[h], (1, t // LANES))).astype(BF16) for h in heads]
        alpha = [jnp.exp(m_prev[h] - m_new[h]) for h in heads]
        pv = [jnp.dot(p[h], v_ref_h[rows, :], preferred_element_type=F32)
              for h, v_ref_h in enumerate((va_ref, vb_ref))]
        for h in heads:
            acc_ref[h] = alpha[h] * acc_ref[h] + pv[h]
            m_ref[h] = m_new[h]

    for h, s_h in enumerate(scores(i, True)):
        s_ref[h] = s_h

    def back(step, carry):
        s_next = scores(i - 1 - step, False)
        accumulate(i - step, [s_ref[h] for h in heads])
        for h in heads:
            s_ref[h] = s_next[h]
        return carry

    lax.fori_loop(0, n_prev, back, 0)
    accumulate(i - n_prev, [s_ref[h] for h in heads])

    a0 = acc_ref[0]
    a1 = acc_ref[1]
    o0 = a0 / a0[:, HEAD_DIM:HEAD_DIM + 1]
    o1 = a1 / a1[:, 0:1]
    o_ref[...] = jnp.where(first, o0, o1).astype(o_ref.dtype)


def _attn_prompt(layer, q, kp, vp, lft):
    t = ATT_T
    nt = N_PROMPT // t
    kv_spec = pl.BlockSpec((1, N_PROMPT, LANES), lambda j, i: (layer, 0, j))
    return pl.pallas_call(
        _attn_prompt_kernel,
        grid=(N_HEADS // 2, nt),
        in_specs=[
            pl.BlockSpec((t, LANES), lambda j, i: (i, j)),
            kv_spec,
            kv_spec,
            pl.BlockSpec((1, 2, N_PROMPT), lambda j, i: (j, 0, 0)),
        ],
        out_specs=pl.BlockSpec((t, LANES), lambda j, i: (i, j)),
        out_shape=jax.ShapeDtypeStruct((N_PROMPT, ATT_WIDTH), BF16),
        scratch_shapes=[
            pltpu.VMEM((2, N_PROMPT), F32),
            pltpu.VMEM((nt, 2, t), F32),
            pltpu.VMEM((2, 1), F32),
            pltpu.VMEM((N_PROMPT, LANES), BF16),
            pltpu.VMEM((N_PROMPT, LANES), BF16),
            pltpu.VMEM((N_PROMPT, LANES), BF16),
            pltpu.VMEM((2, t, LANES), F32),
            pltpu.VMEM((2, t, LANES), F32),
            pltpu.VMEM((2, t, t), F32),
        ],
        compiler_params=_cparams(("parallel", "arbitrary")),
        name="attn_prompt",
    )(q, kp, vp, lft)


SAMPLE_HEAD_GROUP = 4


def _attn_sample_kernel(q_ref, kn_ref, vn_ref, kc_ref, vc_ref, lfc_ref, lfn_ref, o_ref):
    t = q_ref.shape[0]
    q = q_ref[...]
    kn = kn_ref[0].astype(BF16)
    vn = vn_ref[0].astype(BF16)
    xc = lfc_ref[0]
    ckc = xc - _lane_cumsum(xc, reverse=True)
    ckn = _lane_cumsum(lfn_ref[0])[:, :t]
    keep = lax.broadcasted_iota(jnp.int32, (t, t), 1) <= lax.broadcasted_iota(jnp.int32, (t, t), 0)
    dn = (((1,), (1,)), ((), ()))
    outs = []
    for h0 in range(0, N_HEADS, SAMPLE_HEAD_GROUP):
        hs = range(h0, h0 + SAMPLE_HEAD_GROUP)
        sl = [slice(h * HEAD_DIM, (h + 1) * HEAD_DIM) for h in hs]
        kc = [kc_ref[0, 0, h].astype(BF16) for h in hs]
        vc = [vc_ref[0, 0, h].astype(BF16) for h in hs]
        s_c = [jnp.dot(q[:, s], k, preferred_element_type=F32) - ckc[h:h + 1, :]
               for h, s, k in zip(hs, sl, kc)]
        s_n = [jnp.where(keep, lax.dot_general(q[:, s], kn[:, s], dn, preferred_element_type=F32)
                         - ckn[h:h + 1, :], NEG_BIG) for h, s in zip(hs, sl)]
        m = [jnp.maximum(jnp.max(a, axis=-1, keepdims=True), jnp.max(b, axis=-1, keepdims=True))
             for a, b in zip(s_c, s_n)]
        p_c = [jnp.exp(a - mm) for a, mm in zip(s_c, m)]
        p_n = [jnp.exp(b - mm) for b, mm in zip(s_n, m)]
        l = [jnp.sum(a, axis=-1, keepdims=True) + jnp.sum(b, axis=-1, keepdims=True)
             for a, b in zip(p_c, p_n)]
        outs += [(lax.dot_general(a.astype(BF16), v, dn, preferred_element_type=F32)
                  + jnp.dot(b.astype(BF16), vn[:, s], preferred_element_type=F32)) / ll
                 for a, b, v, s, ll in zip(p_c, p_n, vc, sl, l)]
    o_ref[...] = jnp.concatenate(outs, axis=-1).astype(o_ref.dtype)


def _attn_sample(layer, q, ks, vs, cache_k, cache_v, lfc, lfn):
    t = DEC_SEQ
    new = lambda b: (layer, b, 0)
    cache_spec = pl.BlockSpec((1, 1, N_HEADS, HEAD_DIM, PAST_LEN), lambda b: (layer, b, 0, 0, 0))
    return pl.pallas_call(
        _attn_sample_kernel,
        grid=(DEC_BATCH,),
        in_specs=[
            pl.BlockSpec((t, ATT_WIDTH), lambda b: (N_PROMPT // t + b, 0)),
            pl.BlockSpec((1, t, ATT_WIDTH), new),
            pl.BlockSpec((1, t, ATT_WIDTH), new),
            cache_spec,
            cache_spec,
            pl.BlockSpec((1, N_HEADS, PAST_LEN), lambda b: (b, 0, 0)),
            pl.BlockSpec((1, N_HEADS, LANES), lambda b: (b, 0, 0)),
        ],
        out_specs=pl.BlockSpec((t, ATT_WIDTH), lambda b: (b, 0)),
        out_shape=jax.ShapeDtypeStruct((N_SAMPLE, ATT_WIDTH), BF16),
        compiler_params=_cparams(("parallel",)),
        name="attn_sample",
    )(q, ks, vs, cache_k, cache_v, lfc, lfn)


def _ssm_kernel(nseq, nblk, chained, u_ref, wst_ref, win_ref, wca_ref, a_ref, h0_ref, y_ref, hf_ref, sx_ref):
    nlb = 2 * SSM_S // LANES
    lanes = lambda x: [x[:, c * LANES:(c + 1) * LANES] for c in range(nlb)]
    u = u_ref[0]
    for c, blk in enumerate(lanes(jnp.dot(u, wst_ref[0], preferred_element_type=F32))):
        sx_ref[c] = blk
    a = lanes(a_ref[0])
    h0 = lanes(h0_ref[0])
    half = nlb // 2

    def step(a_, x):
        return ([a_[c] * x[c] - a_[c + half] * x[c + half] for c in range(half)]
                + [a_[c] * x[c + half] + a_[c + half] * x[c] for c in range(half)])

    def swap_in(rows, state):
        blk = [sx_ref[c, rows, :] for c in range(nlb)]
        for c in range(nlb):
            sx_ref[c, rows, :] = state[c]
        return blk

    rows_of = lambda j: pl.ds(pl.multiple_of(j * nseq, nseq), nseq)

    def body(j, state):
        blk = swap_in(rows_of(j), state)
        return [n + b for n, b in zip(step(a, state), blk)]

    if chained:
        ends = lax.fori_loop(0, nblk, body, [jnp.zeros((nseq, LANES), F32)] * nlb)
        a_seg = a
        for _ in range(nblk.bit_length() - 1):
            a_seg = step(a_seg, a_seg)
        state, starts = h0, []
        for s in range(nseq):
            starts.append(state)
            state = [n + e[s:s + 1] for n, e in zip(step(a_seg, state), ends)]
        final = state

        def fix(j, corr):
            for c in range(nlb):
                sx_ref[c, rows_of(j), :] = sx_ref[c, rows_of(j), :] + corr[c]
            return step(a, corr)

        lax.fori_loop(0, nblk, fix, [jnp.concatenate([st[c] for st in starts], axis=0) for c in range(nlb)])
    else:
        final = lax.fori_loop(0, nblk, body, h0)
    hf_ref[0] = jnp.concatenate(final, axis=-1)
    s_excl = jnp.concatenate([sx_ref[c] for c in range(nlb)], axis=-1).astype(BF16)
    y_ref[0] = (jnp.dot(u, win_ref[0], preferred_element_type=F32)
                + jnp.dot(s_excl, wca_ref[0], preferred_element_type=F32))


def _ssm(u, wst, win, wca, a, h0, nseq, nblk, chained):
    r = nblk * nseq
    nstate = 1 if chained else nseq
    if chained:
        assert nblk & (nblk - 1) == 0, nblk
    slab = lambda g: (g, 0, 0)
    return pl.pallas_call(
        functools.partial(_ssm_kernel, nseq, nblk, chained),
        grid=(N_GB,),
        in_specs=[
            pl.BlockSpec((1, r, SSM_K), slab),
            pl.BlockSpec((1, SSM_K, 2 * SSM_S), slab),
            pl.BlockSpec((1, SSM_K, SSM_K), slab),
            pl.BlockSpec((1, 2 * SSM_S, SSM_K), slab),
            pl.BlockSpec((1, 1, 2 * SSM_S), slab),
            pl.BlockSpec((1, nstate, 2 * SSM_S), slab),
        ],
        out_specs=[
            pl.BlockSpec((1, r, SSM_K), slab),
            pl.BlockSpec((1, nstate, 2 * SSM_S), slab),
        ],
        out_shape=(
            jax.ShapeDtypeStruct((N_GB, r, SSM_K), F32),
            jax.ShapeDtypeStruct((N_GB, nstate, 2 * SSM_S), F32),
        ),
        scratch_shapes=[pltpu.VMEM((2 * SSM_S // LANES, r, LANES), F32)],
        compiler_params=_cparams(("parallel",)),
        name="ssm_p" if chained else "ssm_s",
    )(u, wst, win, wca, a, h0)


def _ssm_weights(lam_re, lam_im, log_dt, b_re, b_im, c_re, c_im):
    hp = lax.Precision.HIGHEST
    nl = lam_re.shape[0]
    lb = SSM_LB
    dt = jnp.exp(log_dt.astype(F32))[..., None]
    lr, li = lam_re.astype(F32), lam_im.astype(F32)
    mag = jnp.exp(lr * dt)
    a_re = mag * jnp.cos(li * dt)
    a_im = mag * jnp.sin(li * dt)
    nr, ni = a_re - 1.0, a_im
    den = lr * lr + li * li
    f_re = (nr * lr + ni * li) / den
    f_im = (ni * lr - nr * li) / den
    br, bi = b_re.astype(F32), b_im.astype(F32)
    bb_re = f_re[..., None] * br - f_im[..., None] * bi
    bb_im = f_re[..., None] * bi + f_im[..., None] * br
    pr, pi = [jnp.ones_like(a_re)], [jnp.zeros_like(a_re)]
    for _ in range(lb):
        pr, pi = pr + [pr[-1] * a_re - pi[-1] * a_im], pi + [pr[-1] * a_im + pi[-1] * a_re]
    p_re, p_im = jnp.stack(pr), jnp.stack(pi)
    xb_re = p_re[:lb, ..., None] * bb_re - p_im[:lb, ..., None] * bb_im
    xb_im = p_re[:lb, ..., None] * bb_im + p_im[:lb, ..., None] * bb_re
    cr, ci = c_re.astype(F32), c_im.astype(F32)
    kt = (jnp.einsum('lgmp,tlgpn->tlgmn', cr, xb_re, precision=hp)
          - jnp.einsum('lgmp,tlgpn->tlgmn', ci, xb_im, precision=hp))
    ktc = kt.reshape(lb, nl, N_GB, SSM_GB, SSM_GROUP, SSM_GROUP).transpose(1, 2, 0, 3, 5, 4)
    xs = jnp.stack([xb_re[::-1], xb_im[::-1]], axis=1)
    xsc = xs.reshape(lb, 2, nl, N_GB, SSM_GB, STATE_DIM, SSM_GROUP).transpose(2, 3, 0, 1, 4, 6, 5)
    m_re = cr[None] * p_re[1:, :, :, None, :] - ci[None] * p_im[1:, :, :, None, :]
    m_im = cr[None] * p_im[1:, :, :, None, :] + ci[None] * p_re[1:, :, :, None, :]
    ms = jnp.stack([m_re, -m_im], axis=0)
    mcc = ms.reshape(2, lb, nl, N_GB, SSM_GB, SSM_GROUP, STATE_DIM).transpose(2, 3, 0, 1, 4, 6, 5)
    win, wst, wca = _ssm_expand(ktc, xsc, mcc)
    a_lb = jnp.concatenate([p_re[lb].reshape(nl, N_GB, 1, SSM_S),
                            p_im[lb].reshape(nl, N_GB, 1, SSM_S)], axis=-1)
    return wst, win, wca, a_lb


def _ssm_expand_kernel(kt_ref, xs_ref, mc_ref, win_ref, wst_ref, wca_ref, w_ref, bd_ref):
    lb, gb, grp, sd = SSM_LB, SSM_GB, SSM_GROUP, STATE_DIM
    bd_ref[...] = jnp.zeros(bd_ref.shape, F32)
    for tau in range(lb):
        for g in range(gb):
            bd_ref[tau, g * grp:(g + 1) * grp, g * grp:(g + 1) * grp] = kt_ref[0, 0, tau, g]
    w_ref[...] = jnp.zeros(w_ref.shape, F32)
    for r0 in range(lb):
        for r1 in range(r0, lb):
            w_ref[r0 * LANES:(r0 + 1) * LANES, r1 * LANES:(r1 + 1) * LANES] = bd_ref[r1 - r0]
    win_ref[0, 0] = w_ref[...].astype(BF16)

    w_ref[...] = jnp.zeros(w_ref.shape, F32)
    for r0 in range(lb):
        for ri in range(2):
            for g in range(gb):
                rows = slice(r0 * LANES + g * grp, r0 * LANES + (g + 1) * grp)
                cols = slice(ri * SSM_S + g * sd, ri * SSM_S + (g + 1) * sd)
                w_ref[rows, cols] = xs_ref[0, 0, r0, ri, g]
    wst_ref[0, 0] = w_ref[...].astype(BF16)

    w_ref[...] = jnp.zeros(w_ref.shape, F32)
    for ri in range(2):
        for r1 in range(lb):
            for g in range(gb):
                rows = slice(ri * SSM_S + g * sd, ri * SSM_S + (g + 1) * sd)
                cols = slice(r1 * LANES + g * grp, r1 * LANES + (g + 1) * grp)
                w_ref[rows, cols] = mc_ref[0, 0, ri, r1, g]
    wca_ref[0, 0] = w_ref[...].astype(BF16)


def _ssm_expand(ktc, xsc, mcc):
    nl = ktc.shape[0]
    lb, gb, grp, sd = SSM_LB, SSM_GB, SSM_GROUP, STATE_DIM
    slab = lambda l, b: (l, b, 0, 0)
    out = jax.ShapeDtypeStruct((nl, N_GB, SSM_K, SSM_K), BF16)
    return pl.pallas_call(
        _ssm_expand_kernel,
        grid=(nl, N_GB),
        in_specs=[
            pl.BlockSpec((1, 1, lb, gb, grp, grp), lambda l, b: (l, b, 0, 0, 0, 0)),
            pl.BlockSpec((1, 1, lb, 2, gb, grp, sd), lambda l, b: (l, b, 0, 0, 0, 0, 0)),
            pl.BlockSpec((1, 1, 2, lb, gb, sd, grp), lambda l, b: (l, b, 0, 0, 0, 0, 0)),
        ],
        out_specs=[pl.BlockSpec((1, 1, SSM_K, SSM_K), slab)] * 3,
        out_shape=(out, out, out),
        scratch_shapes=[pltpu.VMEM((SSM_K, SSM_K), F32), pltpu.VMEM((lb, LANES, LANES), F32)],
        compiler_params=_cparams(("parallel", "parallel")),
        name="ssm_expand",
    )(ktc, xsc, mcc)


def _merge_kernel(x_ref, oatt_ref, y_ref, s_ref, gate_ref, d_ref, wglu_ref, bglu_ref,
                  pa_ref, pb_ref, wo_ref, gpost_ref, o_ref):
    y = jax.nn.gelu(y_ref[...] + d_ref[...] * s_ref[...])
    glu = jnp.dot(y.astype(BF16), wglu_ref[...], preferred_element_type=F32) + bglu_ref[...]
    o_ssm = (y * jax.nn.sigmoid(glu)).astype(BF16)
    pa = jnp.dot(oatt_ref[...], pa_ref[...], preferred_element_type=F32)
    pb = jnp.dot(o_ssm, pb_ref[...], preferred_element_type=F32)
    gate = gate_ref[...].astype(F32)
    merged = gate[:, :D_MODEL] * pa + gate[:, D_MODEL:] * pb
    m = jnp.dot(merged.astype(BF16), wo_ref[...], preferred_element_type=F32)
    o_ref[...] = x_ref[...] + _rms(m, gpost_ref[...])


def _merge(x, o_att, y_raw, s_in, gate, d, wglu, bglu, pa, pb, wo, gpost):
    n = x.shape[0]
    tm = MERGE_TM
    row = lambda i: (i, 0)
    const = lambda i: (0, 0)
    once = pl.Buffered(1)
    return pl.pallas_call(
        _merge_kernel,
        grid=(n // tm,),
        in_specs=[
            pl.BlockSpec((tm, D_MODEL), row),
            pl.BlockSpec((tm, ATT_WIDTH), row),
            pl.BlockSpec((tm, SSM_WIDTH), row),
            pl.BlockSpec((tm, SSM_WIDTH), row),
            pl.BlockSpec((tm, 2 * D_MODEL), row),
            pl.BlockSpec((1, SSM_WIDTH), const),
            pl.BlockSpec((SSM_WIDTH, SSM_WIDTH), const, pipeline_mode=once),
            pl.BlockSpec((1, SSM_WIDTH), const),
            pl.BlockSpec((ATT_WIDTH, D_MODEL), const, pipeline_mode=once),
            pl.BlockSpec((SSM_WIDTH, D_MODEL), const, pipeline_mode=once),
            pl.BlockSpec((D_MODEL, D_MODEL), const, pipeline_mode=once),
            pl.BlockSpec((1, D_MODEL), const),
        ],
        out_specs=pl.BlockSpec((tm, D_MODEL), row),
        out_shape=jax.ShapeDtypeStruct((n, D_MODEL), F32),
        compiler_params=_cparams(("parallel",)),
        name="merge",
    )(x, o_att, y_raw, s_in, gate, d, wglu, bglu, pa, pb, wo, gpost)


def _to_slabs(s, nseq, nblk):
    t = s.reshape(nseq, nblk, SSM_LB, N_GB, LANES)
    return t.transpose(3, 1, 0, 2, 4).reshape(N_GB, nblk * nseq, SSM_K)


def _from_slabs(y, nseq, nblk):
    t = y.reshape(N_GB, nblk, nseq, SSM_LB, LANES)
    return t.transpose(2, 1, 3, 0, 4).reshape(nseq * nblk * SSM_LB, SSM_WIDTH)


def _state_to_slabs(h_re, h_im):
    nseq = h_re.shape[0]
    r = h_re.astype(F32).reshape(nseq, N_GB, SSM_S).transpose(1, 0, 2)
    i = h_im.astype(F32).reshape(nseq, N_GB, SSM_S).transpose(1, 0, 2)
    return jnp.concatenate([r, i], axis=-1)


def _state_from_slabs(h):
    nseq = h.shape[1]
    r = h[:, :, :SSM_S].transpose(1, 0, 2).reshape(nseq, N_GROUPS, STATE_DIM)
    i = h[:, :, SSM_S:].transpose(1, 0, 2).reshape(nseq, N_GROUPS, STATE_DIM)
    return r, i


def _pad_cols(w, n):
    return jnp.pad(w, ((0, 0), (0, 0), (0, n - w.shape[-1])))


def kernel(x_prompt, x_sample, cache_k, cache_v, cache_logf, state_ssm_re, state_ssm_im, norm_f1_pre, norm_f1_post, norm_mix_pre, norm_mix_post, norm_f2_pre, norm_f2_post, ffn1_w_gate, ffn1_w_up, ffn1_w_down, ffn2_w_gate, ffn2_w_up, ffn2_w_down, w_in, b_forget, b_gate, ssm_lam_re, ssm_lam_im, ssm_log_dt, ssm_b_re, ssm_b_im, ssm_c_re, ssm_c_im, ssm_d, ssm_w_glu, ssm_b_glu, w_proj_attn, w_proj_ssm, w_out):
    ffn_w = []
    for wg, wu, wd in ((ffn1_w_gate, ffn1_w_up, ffn1_w_down), (ffn2_w_gate, ffn2_w_up, ffn2_w_down)):
        ffn_w.append((_pad_cols(wg.astype(BF16), D_FF_PAD), _pad_cols(wu.astype(BF16), D_FF_PAD),
                      jnp.pad(wd.astype(BF16), ((0, 0), (0, D_FF_PAD - D_FF), (0, 0)))))
    a3 = 3 * ATT_WIDTH
    w_main = jnp.concatenate([w_in[:, :, :a3], w_in[:, :, a3 + N_HEADS:]], axis=-1).astype(BF16)
    w_f = _pad_cols(w_in[:, :, a3:a3 + N_HEADS].astype(BF16), LANES)
    b_f = _pad_cols(b_forget.astype(F32)[:, None, :], LANES)
    wglu = ssm_w_glu.astype(BF16)
    wpa = w_proj_attn.astype(BF16)
    wpb = w_proj_ssm.astype(BF16)
    wo = w_out.astype(BF16)
    wst, win, wca, a_lb = _ssm_weights(ssm_lam_re, ssm_lam_im, ssm_log_dt, ssm_b_re, ssm_b_im,
                                       ssm_c_re, ssm_c_im)
    row = lambda v: v.astype(F32)[:, None, :]
    g_f1a, g_f1b, g_ma, g_mb, g_f2a, g_f2b = map(row, (norm_f1_pre, norm_f1_post, norm_mix_pre,
                                                       norm_mix_post, norm_f2_pre, norm_f2_post))
    b_g, d_ssm, b_glu = row(b_gate), row(ssm_d), row(ssm_b_glu)

    x = jnp.concatenate([x_prompt.reshape(N_PROMPT, D_MODEL), x_sample.reshape(N_SAMPLE, D_MODEL)], axis=0)
    nblk_p = N_PROMPT // SSM_LB // SSM_SEG
    nblk_s = DEC_SEQ // SSM_LB
    h0_p = jnp.zeros((N_GB, 1, 2 * SSM_S), F32)
    lfc_all = cache_logf.astype(F32).transpose(0, 1, 3, 2)
    cache_k2 = cache_k.transpose(0, 1, 3, 4, 2)
    cache_v2 = cache_v.transpose(0, 1, 3, 4, 2)
    kp = jnp.zeros((DEPTH, N_PROMPT, ATT_WIDTH), F32)
    vp = jnp.zeros((DEPTH, N_PROMPT, ATT_WIDTH), F32)
    ks = jnp.zeros((DEPTH, N_SAMPLE, ATT_WIDTH), F32)
    vs = jnp.zeros((DEPTH, N_SAMPLE, ATT_WIDTH), F32)
    outs = [[] for _ in range(6)]
    for l in range(DEPTH):
        x = _ffn(x, g_f1a[l], g_f1b[l], *[w[l] for w in ffn_w[0]])
        q, s_in, gate, lf, kp, vp, ks, vs = _inproj(
            l, x, g_ma[l], w_main[l], w_f[l], b_f[l], b_g[l], kp, vp, ks, vs)
        logf = lf[:, :N_HEADS]
        logf_p = logf[:N_PROMPT]
        logf_s = logf[N_PROMPT:].reshape(DEC_BATCH, DEC_SEQ, N_HEADS)

        o_p = _attn_prompt(l, q, kp, vp, logf_p.T.reshape(N_HEADS // 2, 2, N_PROMPT))
        lfn = jnp.pad(logf_s.transpose(0, 2, 1), ((0, 0), (0, 0), (0, LANES - DEC_SEQ)))
        o_s = _attn_sample(l, q, ks, vs, cache_k2, cache_v2, lfc_all[l], lfn)
        o_att = jnp.concatenate([o_p, o_s], axis=0)

        u = s_in.astype(BF16)
        y_p, h_p = _ssm(_to_slabs(u[:N_PROMPT], SSM_SEG, nblk_p), wst[l], win[l], wca[l], a_lb[l],
                        h0_p, SSM_SEG, nblk_p, True)
        h0_s = _state_to_slabs(state_ssm_re[l], state_ssm_im[l])
        y_s, h_s = _ssm(_to_slabs(u[N_PROMPT:], DEC_BATCH, nblk_s), wst[l], win[l], wca[l], a_lb[l],
                        h0_s, DEC_BATCH, nblk_s, False)
        y_raw = jnp.concatenate([_from_slabs(y_p, SSM_SEG, nblk_p), _from_slabs(y_s, DEC_BATCH, nblk_s)], axis=0)

        x = _merge(x, o_att, y_raw, s_in, gate, d_ssm[l], wglu[l], b_glu[l], wpa[l], wpb[l], wo[l], g_mb[l])
        x = _ffn(x, g_f2a[l], g_f2b[l], *[w[l] for w in ffn_w[1]])

        hp_re, hp_im = _state_from_slabs(h_p)
        hs_re, hs_im = _state_from_slabs(h_s)
        for o, v in zip(outs, (logf_p.reshape(1, SEQ, N_HEADS), hp_re, hp_im, logf_s, hs_re, hs_im)):
            o.append(v)
    lf_p, hr_p, hi_p, lf_s, hr_s, hi_s = [jnp.stack(o) for o in outs]
    return (x[:N_PROMPT].reshape(1, SEQ, D_MODEL), x[N_PROMPT:].reshape(DEC_BATCH, DEC_SEQ, D_MODEL),
            kp.reshape(DEPTH, 1, SEQ, N_HEADS, HEAD_DIM), vp.reshape(DEPTH, 1, SEQ, N_HEADS, HEAD_DIM),
            lf_p, hr_p, hi_p,
            ks.reshape(DEPTH, DEC_BATCH, DEC_SEQ, N_HEADS, HEAD_DIM),
            vs.reshape(DEPTH, DEC_BATCH, DEC_SEQ, N_HEADS, HEAD_DIM),
            lf_s, hr_s, hi_s)
```

```python
import functools

import jax
import jax.numpy as jnp
from jax import lax
from jax.experimental import pallas as pl
from jax.experimental.pallas import tpu as pltpu

F32 = jnp.float32
BF16 = jnp.bfloat16

D_MODEL = 2048
SEQ = 8192
DEPTH = 4
DEC_BATCH = 32
DEC_SEQ = 32
PAST_LEN = 1024
N_HEADS = 16
HEAD_DIM = 64
ATT_WIDTH = N_HEADS * HEAD_DIM
SSM_WIDTH = D_MODEL // 2
SSM_GROUP = 16
N_GROUPS = SSM_WIDTH // SSM_GROUP
STATE_DIM = 64
D_FF = 5504
EPS = 1e-6

N_PROMPT = SEQ
N_SAMPLE = DEC_BATCH * DEC_SEQ
N_TOK = N_PROMPT + N_SAMPLE

LANES = 128
D_FF_PAD = 5632
FFN_TM = 512
FFN_TF = 512
INPROJ_TM = 512
INPROJ_TN = 1024
ATT_T = 256
ATT_QH = 2
MERGE_TM = 256
SSM_LB = 8
SSM_GB = 8
N_GB = N_GROUPS // SSM_GB
SSM_K = SSM_LB * LANES
SSM_S = SSM_GB * STATE_DIM
SSM_SEG = 8
NEG_BIG = -1e30
SKIP_GAP = 110.0
NORM_SLACK = 2.0 * 1.01
VMEM_LIMIT = 56 * 1024 * 1024


def _cparams(sem):
    return pltpu.CompilerParams(dimension_semantics=sem, vmem_limit_bytes=VMEM_LIMIT)


def _rms(x, g):
    ms = jnp.mean(x * x, axis=-1, keepdims=True)
    return x * lax.rsqrt(ms + EPS) * g


def _ffn_kernel(x_ref, gpre_ref, gpost_ref, wg_ref, wu_ref, wd_ref, o_ref, xn_ref):
    f = pl.program_id(1)
    nf = pl.num_programs(1)

    @pl.when(f == 0)
    def _():
        xn_ref[...] = _rms(x_ref[...], gpre_ref[...]).astype(BF16)
        o_ref[...] = jnp.zeros(o_ref.shape, F32)

    xn = xn_ref[...]
    g = jnp.dot(xn, wg_ref[...], preferred_element_type=F32)
    u = jnp.dot(xn, wu_ref[...], preferred_element_type=F32)
    h = (g * jax.nn.sigmoid(g) * u).astype(BF16)
    o_ref[...] += jnp.dot(h, wd_ref[...], preferred_element_type=F32)

    @pl.when(f == nf - 1)
    def _():
        o_ref[...] = x_ref[...] + 0.5 * _rms(o_ref[...], gpost_ref[...])


def _ffn(x, gpre, gpost, wg, wu, wd):
    n = x.shape[0]
    grid = (n // FFN_TM, D_FF_PAD // FFN_TF)
    return pl.pallas_call(
        _ffn_kernel,
        grid=grid,
        in_specs=[
            pl.BlockSpec((FFN_TM, D_MODEL), lambda i, f: (i, 0)),
            pl.BlockSpec((1, D_MODEL), lambda i, f: (0, 0)),
            pl.BlockSpec((1, D_MODEL), lambda i, f: (0, 0)),
            pl.BlockSpec((D_MODEL, FFN_TF), lambda i, f: (0, f)),
            pl.BlockSpec((D_MODEL, FFN_TF), lambda i, f: (0, f)),
            pl.BlockSpec((FFN_TF, D_MODEL), lambda i, f: (f, 0)),
        ],
        out_specs=pl.BlockSpec((FFN_TM, D_MODEL), lambda i, f: (i, 0)),
        out_shape=jax.ShapeDtypeStruct((n, D_MODEL), F32),
        scratch_shapes=[pltpu.VMEM((FFN_TM, D_MODEL), BF16)],
        compiler_params=_cparams(("parallel", "arbitrary")),
        name="ffn",
    )(x, gpre, gpost, wg, wu, wd)


def _log_sigmoid(x):
    return jnp.minimum(x, 0.0) - jnp.log1p(jnp.exp(-jnp.abs(x)))


N_PTILES = N_PROMPT // INPROJ_TM


def _inproj_kernel(x_ref, g_ref, w_ref, wf_ref, bf_ref, bg_ref, kp_in, vp_in, ks_in, vs_in,
                   q_ref, s_ref, gate_ref, lf_ref, kp_ref, vp_ref, ks_ref, vs_ref,
                   xn_ref):
    del kp_in, vp_in, ks_in, vs_in
    i = pl.program_id(0)
    j = pl.program_id(1)

    @pl.when(j == 0)
    def _():
        xn = _rms(x_ref[...], g_ref[...]).astype(BF16)
        xn_ref[...] = xn
        fl = jnp.dot(xn, wf_ref[...], preferred_element_type=F32) + bf_ref[...]
        lf_ref[...] = _log_sigmoid(fl)

    z = jnp.dot(xn_ref[...], w_ref[...], preferred_element_type=F32)

    @pl.when(j == 0)
    def _():
        q_ref[...] = (z * (HEAD_DIM ** -0.5)).astype(BF16)

    for jj, (p_ref, s_out) in ((1, (kp_ref, ks_ref)), (2, (vp_ref, vs_ref))):
        @pl.when((j == jj) & (i < N_PTILES))
        def _(p_ref=p_ref):
            p_ref[0] = z

        @pl.when((j == jj) & (i >= N_PTILES))
        def _(s_out=s_out):
            s_out[0] = z

    @pl.when(j == 3)
    def _():
        s_ref[...] = z

    for jj in range(2 * D_MODEL // INPROJ_TN):
        @pl.when(j == 4 + jj)
        def _(jj=jj):
            sl = slice(jj * INPROJ_TN, (jj + 1) * INPROJ_TN)
            gate_ref[:, sl] = jax.nn.sigmoid(z + bg_ref[:, sl]).astype(BF16)


def _inproj(layer, x, g, w, wf, bf, bg, kp, vp, ks, vs):
    n = x.shape[0]
    tm = INPROJ_TM
    ncol = w.shape[1] // INPROJ_TN
    row = lambda i, j: (i, 0)
    const = lambda i, j: (0, 0)
    p_map = lambda i, j: (layer, jnp.minimum(i, N_PTILES - 1), 0)
    s_map = lambda i, j: (layer, jnp.maximum(i - N_PTILES, 0), 0)
    tok_outs = (
        jax.ShapeDtypeStruct((n, ATT_WIDTH), BF16),
        jax.ShapeDtypeStruct((n, SSM_WIDTH), F32),
        jax.ShapeDtypeStruct((n, 2 * D_MODEL), BF16),
        jax.ShapeDtypeStruct((n, LANES), F32),
    )
    kv_outs = tuple(jax.ShapeDtypeStruct(a.shape, a.dtype) for a in (kp, vp, ks, vs))
    any_spec = pl.BlockSpec(memory_space=pl.ANY)
    return pl.pallas_call(
        _inproj_kernel,
        grid=(n // tm, ncol),
        in_specs=[
            pl.BlockSpec((tm, D_MODEL), row),
            pl.BlockSpec((1, D_MODEL), const),
            pl.BlockSpec((D_MODEL, INPROJ_TN), lambda i, j: (0, j)),
            pl.BlockSpec((D_MODEL, LANES), const),
            pl.BlockSpec((1, LANES), const),
            pl.BlockSpec((1, 2 * D_MODEL), const),
            any_spec, any_spec, any_spec, any_spec,
        ],
        out_specs=[pl.BlockSpec((tm, o.shape[1]), row) for o in tok_outs] + [
            pl.BlockSpec((1, tm, ATT_WIDTH), p_map), pl.BlockSpec((1, tm, ATT_WIDTH), p_map),
            pl.BlockSpec((1, tm, ATT_WIDTH), s_map), pl.BlockSpec((1, tm, ATT_WIDTH), s_map),
        ],
        out_shape=tok_outs + kv_outs,
        input_output_aliases={6: 4, 7: 5, 8: 6, 9: 7},
        scratch_shapes=[pltpu.VMEM((tm, D_MODEL), BF16)],
        compiler_params=_cparams(("arbitrary", "arbitrary")),
        name="inproj",
    )(x, g, w, wf, bf, bg, kp, vp, ks, vs)


def _lane_cumsum(x, reverse=False):
    n = x.shape[-1]
    pos = lax.broadcasted_iota(jnp.int32, x.shape, x.ndim - 1)
    sh = 1
    while sh < n:
        if reverse:
            x = x + jnp.where(pos < n - sh, pltpu.roll(x, n - sh, axis=x.ndim - 1), 0.0)
        else:
            x = x + jnp.where(pos >= sh, pltpu.roll(x, sh, axis=x.ndim - 1), 0.0)
        sh *= 2
    return x


def _head_sq_norm_max(x, first):
    sq = x * x
    a = jnp.max(jnp.sum(jnp.where(first, sq, 0.0), axis=-1, keepdims=True), axis=0, keepdims=True)
    b = jnp.max(jnp.sum(jnp.where(first, 0.0, sq), axis=-1, keepdims=True), axis=0, keepdims=True)
    return jnp.concatenate([a, b], axis=0)


def _attn_prompt_kernel(q_ref, kf_ref, vf_ref, lft_ref, o_ref,
                        c_ref, ct_ref, kn_ref, k_ref, va_ref, vb_ref, m_ref, acc_ref, s_ref):
    i = pl.program_id(1)
    t = ATT_T
    s_len = k_ref.shape[0]
    nt = s_len // t
    lane = lax.broadcasted_iota(jnp.int32, (t, LANES), 1)
    first = lane < HEAD_DIM

    @pl.when(i == 0)
    def _():
        c = _lane_cumsum(lft_ref[0])
        c_ref[...] = c
        for jj in range(nt):
            ct_ref[jj] = c[:, jj * t:(jj + 1) * t]

        def prep(r, kn):
            rows = pl.ds(pl.multiple_of(r * t, t), t)
            kb = kf_ref[0, rows, :].astype(BF16)
            k_ref[rows, :] = kb
            kn = jnp.maximum(kn, _head_sq_norm_max(kb.astype(F32), first))
            v = vf_ref[0, rows, :]
            va_ref[rows, :] = jnp.where(first, v, jnp.where(lane == HEAD_DIM, 1.0, 0.0)).astype(BF16)
            vb_ref[rows, :] = jnp.where(first, jnp.where(lane == 0, 1.0, 0.0), v).astype(BF16)
            return kn

        kn_ref[...] = jnp.sqrt(lax.fori_loop(0, nt, prep, jnp.zeros((2, 1), F32)))

    nh = ATT_QH
    all_chains = list(range(2 * nh))
    q = q_ref[...]
    zero = jnp.zeros((t, LANES), q.dtype)
    qc = []
    for a in range(nh):
        qa = q[a * t:(a + 1) * t]
        qc += [jnp.where(first, qa, zero), jnp.where(first, zero, qa)]

    first_q = lax.broadcasted_iota(jnp.int32, q.shape, 1) < HEAD_DIM
    qn = jnp.sqrt(_head_sq_norm_max(q.astype(F32), first_q))
    c = c_ref[...]
    pos = lax.broadcasted_iota(jnp.int32, c.shape, 1)
    base = i * nh
    q0 = base * t
    c_first = jnp.max(jnp.where(pos >= q0, c, -jnp.inf), axis=-1, keepdims=True)
    thr = c_first + NORM_SLACK * qn * kn_ref[...] + SKIP_GAP
    need = jnp.where(pos < q0, jnp.where(c <= thr, 1.0, 0.0), 0.0)
    n_keys = jnp.max(jnp.sum(need, axis=-1, keepdims=True)).astype(jnp.int32)
    n_prev = lax.div(n_keys + (t - 1), t)

    m_ref[...] = jnp.full(m_ref.shape, NEG_BIG, F32)
    acc_ref[...] = jnp.zeros(acc_ref.shape, F32)
    keep = lax.broadcasted_iota(jnp.int32, (t, t), 1) <= lax.broadcasted_iota(jnp.int32, (t, t), 0)
    v_refs = (va_ref, vb_ref)

    def scores(j, chains, masked=()):
        k = k_ref[pl.ds(pl.multiple_of(j * t, t), t), :]
        ck = ct_ref[j]
        s = {}
        for ch in chains:
            h = ch % 2
            s_ch = lax.dot_general(qc[ch], k, (((1,), (1,)), ((), ())),
                                   preferred_element_type=F32) - ck[h:h + 1, :]
            s[ch] = jnp.where(keep, s_ch, NEG_BIG) if ch in masked else s_ch
        return s

    def accumulate(j, chains):
        rows = pl.ds(pl.multiple_of(j * t, t), t)
        s = {ch: s_ref[ch] for ch in chains}
        m_prev = {ch: m_ref[ch] for ch in chains}
        m_new = {ch: jnp.maximum(m_prev[ch], jnp.max(s[ch], axis=-1, keepdims=True)) for ch in chains}
        p = {ch: jnp.exp(s[ch] - jnp.tile(m_new[ch], (1, t // LANES))).astype(BF16) for ch in chains}
        alpha = {ch: jnp.exp(m_prev[ch] - m_new[ch]) for ch in chains}
        pv = {ch: jnp.dot(p[ch], v_refs[ch % 2][rows, :], preferred_element_type=F32) for ch in chains}
        for ch in chains:
            acc_ref[ch] = alpha[ch] * acc_ref[ch] + pv[ch]
            m_ref[ch] = m_new[ch]

    def park(s):
        for ch, s_ch in s.items():
            s_ref[ch] = s_ch

    own = [(b, [2 * a + h for a in range(b, nh) for h in range(2)], [2 * b, 2 * b + 1])
           for b in reversed(range(nh))]
    park(scores(base + own[0][0], own[0][1], own[0][2]))
    for (b_cur, ch_cur, _), (b_nxt, ch_nxt, mk_nxt) in zip(own[:-1], own[1:]):
        s_next = scores(base + b_nxt, ch_nxt, mk_nxt)
        accumulate(base + b_cur, ch_cur)
        park(s_next)

    def back(step, carry):
        s_next = scores(base - 1 - step, all_chains)
        accumulate(base - step, all_chains)
        park(s_next)
        return carry

    lax.fori_loop(0, n_prev, back, 0)
    accumulate(base - n_prev, all_chains)

    for a in range(nh):
        a0 = acc_ref[2 * a]
        a1 = acc_ref[2 * a + 1]
        o0 = a0 / a0[:, HEAD_DIM:HEAD_DIM + 1]
        o1 = a1 / a1[:, 0:1]
        o_ref[a * t:(a + 1) * t, :] = jnp.where(first, o0, o1).astype(o_ref.dtype)


def _attn_prompt(layer, q, kp, vp, lft):
    t = ATT_T
    nt = N_PROMPT // t
    tq = ATT_QH * t
    kv_spec = pl.BlockSpec((1, N_PROMPT, LANES), lambda j, i: (layer, 0, j))
    return pl.pallas_call(
        _attn_prompt_kernel,
        grid=(N_HEADS // 2, N_PROMPT // tq),
        in_specs=[
            pl.BlockSpec((tq, LANES), lambda j, i: (i, j)),
            kv_spec,
            kv_spec,
            pl.BlockSpec((1, 2, N_PROMPT), lambda j, i: (j, 0, 0)),
        ],
        out_specs=pl.BlockSpec((tq, LANES), lambda j, i: (i, j)),
        out_shape=jax.ShapeDtypeStruct((N_PROMPT, ATT_WIDTH), BF16),
        scratch_shapes=[
            pltpu.VMEM((2, N_PROMPT), F32),
            pltpu.VMEM((nt, 2, t), F32),
            pltpu.VMEM((2, 1), F32),
            pltpu.VMEM((N_PROMPT, LANES), BF16),
            pltpu.VMEM((N_PROMPT, LANES), BF16),
            pltpu.VMEM((N_PROMPT, LANES), BF16),
            pltpu.VMEM((2 * ATT_QH, t, LANES), F32),
            pltpu.VMEM((2 * ATT_QH, t, LANES), F32),
            pltpu.VMEM((2 * ATT_QH, t, t), F32),
        ],
        compiler_params=_cparams(("parallel", "arbitrary")),
        name="attn_prompt",
    )(q, kp, vp, lft)


SAMPLE_HEAD_GROUP = 4


def _attn_sample_kernel(q_ref, kn_ref, vn_ref, kc_ref, vc_ref, lfc_ref, lfn_ref, o_ref):
    t = q_ref.shape[0]
    q = q_ref[...]
    kn = kn_ref[0].astype(BF16)
    vn = vn_ref[0].astype(BF16)
    xc = lfc_ref[0]
    ckc = xc - _lane_cumsum(xc, reverse=True)
    ckn = _lane_cumsum(lfn_ref[0])[:, :t]
    keep = lax.broadcasted_iota(jnp.int32, (t, t), 1) <= lax.broadcasted_iota(jnp.int32, (t, t), 0)
    dn = (((1,), (1,)), ((), ()))
    outs = []
    for h0 in range(0, N_HEADS, SAMPLE_HEAD_GROUP):
        hs = range(h0, h0 + SAMPLE_HEAD_GROUP)
        sl = [slice(h * HEAD_DIM, (h + 1) * HEAD_DIM) for h in hs]
        kc = [kc_ref[0, 0, h].astype(BF16) for h in hs]
        vc = [vc_ref[0, 0, h].astype(BF16) for h in hs]
        s_c = [jnp.dot(q[:, s], k, preferred_element_type=F32) - ckc[h:h + 1, :]
               for h, s, k in zip(hs, sl, kc)]
        s_n = [jnp.where(keep, lax.dot_general(q[:, s], kn[:, s], dn, preferred_element_type=F32)
                         - ckn[h:h + 1, :], NEG_BIG) for h, s in zip(hs, sl)]
        m = [jnp.maximum(jnp.max(a, axis=-1, keepdims=True), jnp.max(b, axis=-1, keepdims=True))
             for a, b in zip(s_c, s_n)]
        p_c = [jnp.exp(a - mm) for a, mm in zip(s_c, m)]
        p_n = [jnp.exp(b - mm) for b, mm in zip(s_n, m)]
        l = [jnp.sum(a, axis=-1, keepdims=True) + jnp.sum(b, axis=-1, keepdims=True)
             for a, b in zip(p_c, p_n)]
        outs += [(lax.dot_general(a.astype(BF16), v, dn, preferred_element_type=F32)
                  + jnp.dot(b.astype(BF16), vn[:, s], preferred_element_type=F32)) / ll
                 for a, b, v, s, ll in zip(p_c, p_n, vc, sl, l)]
    o_ref[...] = jnp.concatenate(outs, axis=-1).astype(o_ref.dtype)


def _attn_sample(layer, q, ks, vs, cache_k, cache_v, lfc, lfn):
    t = DEC_SEQ
    new = lambda b: (layer, b, 0)
    cache_spec = pl.BlockSpec((1, 1, N_HEADS, HEAD_DIM, PAST_LEN), lambda b: (layer, b, 0, 0, 0))
    return pl.pallas_call(
        _attn_sample_kernel,
        grid=(DEC_BATCH,),
        in_specs=[
            pl.BlockSpec((t, ATT_WIDTH), lambda b: (N_PROMPT // t + b, 0)),
            pl.BlockSpec((1, t, ATT_WIDTH), new),
            pl.BlockSpec((1, t, ATT_WIDTH), new),
            cache_spec,
            cache_spec,
            pl.BlockSpec((1, N_HEADS, PAST_LEN), lambda b: (b, 0, 0)),
            pl.BlockSpec((1, N_HEADS, LANES), lambda b: (b, 0, 0)),
        ],
        out_specs=pl.BlockSpec((t, ATT_WIDTH), lambda b: (b, 0)),
        out_shape=jax.ShapeDtypeStruct((N_SAMPLE, ATT_WIDTH), BF16),
        compiler_params=_cparams(("parallel",)),
        name="attn_sample",
    )(q, ks, vs, cache_k, cache_v, lfc, lfn)


def _ssm_kernel(nseq, nblk, chained, u_ref, wst_ref, win_ref, wca_ref, a_ref, h0_ref, y_ref, hf_ref, sx_ref):
    nlb = 2 * SSM_S // LANES
    lanes = lambda x: [x[:, c * LANES:(c + 1) * LANES] for c in range(nlb)]
    u = u_ref[0]
    for c, blk in enumerate(lanes(jnp.dot(u, wst_ref[0], preferred_element_type=F32))):
        sx_ref[c] = blk
    a = lanes(a_ref[0])
    h0 = lanes(h0_ref[0])
    half = nlb // 2

    def step(a_, x):
        return ([a_[c] * x[c] - a_[c + half] * x[c + half] for c in range(half)]
                + [a_[c] * x[c + half] + a_[c + half] * x[c] for c in range(half)])

    def swap_in(rows, state):
        blk = [sx_ref[c, rows, :] for c in range(nlb)]
        for c in range(nlb):
            sx_ref[c, rows, :] = state[c]
        return blk

    rows_of = lambda j: pl.ds(pl.multiple_of(j * nseq, nseq), nseq)

    def body(j, state):
        blk = swap_in(rows_of(j), state)
        return [n + b for n, b in zip(step(a, state), blk)]

    if chained:
        ends = lax.fori_loop(0, nblk, body, [jnp.zeros((nseq, LANES), F32)] * nlb)
        a_seg = a
        for _ in range(nblk.bit_length() - 1):
            a_seg = step(a_seg, a_seg)
        state, starts = h0, []
        for s in range(nseq):
            starts.append(state)
            state = [n + e[s:s + 1] for n, e in zip(step(a_seg, state), ends)]
        final = state

        def fix(j, corr):
            for c in range(nlb):
                sx_ref[c, rows_of(j), :] = sx_ref[c, rows_of(j), :] + corr[c]
            return step(a, corr)

        lax.fori_loop(0, nblk, fix, [jnp.concatenate([st[c] for st in starts], axis=0) for c in range(nlb)])
    else:
        final = lax.fori_loop(0, nblk, body, h0)
    hf_ref[0] = jnp.concatenate(final, axis=-1)
    s_excl = jnp.concatenate([sx_ref[c] for c in range(nlb)], axis=-1).astype(BF16)
    y_ref[0] = (jnp.dot(u, win_ref[0], preferred_element_type=F32)
                + jnp.dot(s_excl, wca_ref[0], preferred_element_type=F32))


def _ssm(u, wst, win, wca, a, h0, nseq, nblk, chained):
    r = nblk * nseq
    nstate = 1 if chained else nseq
    if chained:
        assert nblk & (nblk - 1) == 0, nblk
    slab = lambda g: (g, 0, 0)
    return pl.pallas_call(
        functools.partial(_ssm_kernel, nseq, nblk, chained),
        grid=(N_GB,),
        in_specs=[
            pl.BlockSpec((1, r, SSM_K), slab),
            pl.BlockSpec((1, SSM_K, 2 * SSM_S), slab),
            pl.BlockSpec((1, SSM_K, SSM_K), slab),
            pl.BlockSpec((1, 2 * SSM_S, SSM_K), slab),
            pl.BlockSpec((1, 1, 2 * SSM_S), slab),
            pl.BlockSpec((1, nstate, 2 * SSM_S), slab),
        ],
        out_specs=[
            pl.BlockSpec((1, r, SSM_K), slab),
            pl.BlockSpec((1, nstate, 2 * SSM_S), slab),
        ],
        out_shape=(
            jax.ShapeDtypeStruct((N_GB, r, SSM_K), F32),
            jax.ShapeDtypeStruct((N_GB, nstate, 2 * SSM_S), F32),
        ),
        scratch_shapes=[pltpu.VMEM((2 * SSM_S // LANES, r, LANES), F32)],
        compiler_params=_cparams(("parallel",)),
        name="ssm_p" if chained else "ssm_s",
    )(u, wst, win, wca, a, h0)


def _ssm_weights(lam_re, lam_im, log_dt, b_re, b_im, c_re, c_im):
    hp = lax.Precision.HIGHEST
    nl = lam_re.shape[0]
    lb = SSM_LB
    dt = jnp.exp(log_dt.astype(F32))[..., None]
    lr, li = lam_re.astype(F32), lam_im.astype(F32)
    mag = jnp.exp(lr * dt)
    a_re = mag * jnp.cos(li * dt)
    a_im = mag * jnp.sin(li * dt)
    nr, ni = a_re - 1.0, a_im
    den = lr * lr + li * li
    f_re = (nr * lr + ni * li) / den
    f_im = (ni * lr - nr * li) / den
    br, bi = b_re.astype(F32), b_im.astype(F32)
    bb_re = f_re[..., None] * br - f_im[..., None] * bi
    bb_im = f_re[..., None] * bi + f_im[..., None] * br
    pr, pi = [jnp.ones_like(a_re)], [jnp.zeros_like(a_re)]
    for _ in range(lb):
        pr, pi = pr + [pr[-1] * a_re - pi[-1] * a_im], pi + [pr[-1] * a_im + pi[-1] * a_re]
    p_re, p_im = jnp.stack(pr), jnp.stack(pi)
    xb_re = p_re[:lb, ..., None] * bb_re - p_im[:lb, ..., None] * bb_im
    xb_im = p_re[:lb, ..., None] * bb_im + p_im[:lb, ..., None] * bb_re
    cr, ci = c_re.astype(F32), c_im.astype(F32)
    kt = (jnp.einsum('lgmp,tlgpn->tlgmn', cr, xb_re, precision=hp)
          - jnp.einsum('lgmp,tlgpn->tlgmn', ci, xb_im, precision=hp))
    ktc = kt.reshape(lb, nl, N_GB, SSM_GB, SSM_GROUP, SSM_GROUP).transpose(1, 2, 0, 3, 5, 4)
    xs = jnp.stack([xb_re[::-1], xb_im[::-1]], axis=1)
    xsc = xs.reshape(lb, 2, nl, N_GB, SSM_GB, STATE_DIM, SSM_GROUP).transpose(2, 3, 0, 1, 4, 6, 5)
    m_re = cr[None] * p_re[1:, :, :, None, :] - ci[None] * p_im[1:, :, :, None, :]
    m_im = cr[None] * p_im[1:, :, :, None, :] + ci[None] * p_re[1:, :, :, None, :]
    ms = jnp.stack([m_re, -m_im], axis=0)
    mcc = ms.reshape(2, lb, nl, N_GB, SSM_GB, SSM_GROUP, STATE_DIM).transpose(2, 3, 0, 1, 4, 6, 5)
    win, wst, wca = _ssm_expand(ktc, xsc, mcc)
    a_lb = jnp.concatenate([p_re[lb].reshape(nl, N_GB, 1, SSM_S),
                            p_im[lb].reshape(nl, N_GB, 1, SSM_S)], axis=-1)
    return wst, win, wca, a_lb


def _ssm_expand_kernel(kt_ref, xs_ref, mc_ref, win_ref, wst_ref, wca_ref, w_ref, bd_ref):
    lb, gb, grp, sd = SSM_LB, SSM_GB, SSM_GROUP, STATE_DIM
    bd_ref[...] = jnp.zeros(bd_ref.shape, F32)
    for tau in range(lb):
        for g in range(gb):
            bd_ref[tau, g * grp:(g + 1) * grp, g * grp:(g + 1) * grp] = kt_ref[0, 0, tau, g]
    w_ref[...] = jnp.zeros(w_ref.shape, F32)
    for r0 in range(lb):
        for r1 in range(r0, lb):
            w_ref[r0 * LANES:(r0 + 1) * LANES, r1 * LANES:(r1 + 1) * LANES] = bd_ref[r1 - r0]
    win_ref[0, 0] = w_ref[...].astype(BF16)

    w_ref[...] = jnp.zeros(w_ref.shape, F32)
    for r0 in range(lb):
        for ri in range(2):
            for g in range(gb):
                rows = slice(r0 * LANES + g * grp, r0 * LANES + (g + 1) * grp)
                cols = slice(ri * SSM_S + g * sd, ri * SSM_S + (g + 1) * sd)
                w_ref[rows, cols] = xs_ref[0, 0, r0, ri, g]
    wst_ref[0, 0] = w_ref[...].astype(BF16)

    w_ref[...] = jnp.zeros(w_ref.shape, F32)
    for ri in range(2):
        for r1 in range(lb):
            for g in range(gb):
                rows = slice(ri * SSM_S + g * sd, ri * SSM_S + (g + 1) * sd)
                cols = slice(r1 * LANES + g * grp, r1 * LANES + (g + 1) * grp)
                w_ref[rows, cols] = mc_ref[0, 0, ri, r1, g]
    wca_ref[0, 0] = w_ref[...].astype(BF16)


def _ssm_expand(ktc, xsc, mcc):
    nl = ktc.shape[0]
    lb, gb, grp, sd = SSM_LB, SSM_GB, SSM_GROUP, STATE_DIM
    slab = lambda l, b: (l, b, 0, 0)
    out = jax.ShapeDtypeStruct((nl, N_GB, SSM_K, SSM_K), BF16)
    return pl.pallas_call(
        _ssm_expand_kernel,
        grid=(nl, N_GB),
        in_specs=[
            pl.BlockSpec((1, 1, lb, gb, grp, grp), lambda l, b: (l, b, 0, 0, 0, 0)),
            pl.BlockSpec((1, 1, lb, 2, gb, grp, sd), lambda l, b: (l, b, 0, 0, 0, 0, 0)),
            pl.BlockSpec((1, 1, 2, lb, gb, sd, grp), lambda l, b: (l, b, 0, 0, 0, 0, 0)),
        ],
        out_specs=[pl.BlockSpec((1, 1, SSM_K, SSM_K), slab)] * 3,
        out_shape=(out, out, out),
        scratch_shapes=[pltpu.VMEM((SSM_K, SSM_K), F32), pltpu.VMEM((lb, LANES, LANES), F32)],
        compiler_params=_cparams(("parallel", "parallel")),
        name="ssm_expand",
    )(ktc, xsc, mcc)


def _merge_kernel(x_ref, oatt_ref, y_ref, s_ref, gate_ref, d_ref, wglu_ref, bglu_ref,
                  pa_ref, pb_ref, wo_ref, gpost_ref, o_ref):
    y = jax.nn.gelu(y_ref[...] + d_ref[...] * s_ref[...])
    glu = jnp.dot(y.astype(BF16), wglu_ref[...], preferred_element_type=F32) + bglu_ref[...]
    o_ssm = (y * jax.nn.sigmoid(glu)).astype(BF16)
    pa = jnp.dot(oatt_ref[...], pa_ref[...], preferred_element_type=F32)
    pb = jnp.dot(o_ssm, pb_ref[...], preferred_element_type=F32)
    gate = gate_ref[...].astype(F32)
    merged = gate[:, :D_MODEL] * pa + gate[:, D_MODEL:] * pb
    m = jnp.dot(merged.astype(BF16), wo_ref[...], preferred_element_type=F32)
    o_ref[...] = x_ref[...] + _rms(m, gpost_ref[...])


def _merge(x, o_att, y_raw, s_in, gate, d, wglu, bglu, pa, pb, wo, gpost):
    n = x.shape[0]
    tm = MERGE_TM
    row = lambda i: (i, 0)
    const = lambda i: (0, 0)
    once = pl.Buffered(1)
    return pl.pallas_call(
        _merge_kernel,
        grid=(n // tm,),
        in_specs=[
            pl.BlockSpec((tm, D_MODEL), row),
            pl.BlockSpec((tm, ATT_WIDTH), row),
            pl.BlockSpec((tm, SSM_WIDTH), row),
            pl.BlockSpec((tm, SSM_WIDTH), row),
            pl.BlockSpec((tm, 2 * D_MODEL), row),
            pl.BlockSpec((1, SSM_WIDTH), const),
            pl.BlockSpec((SSM_WIDTH, SSM_WIDTH), const, pipeline_mode=once),
            pl.BlockSpec((1, SSM_WIDTH), const),
            pl.BlockSpec((ATT_WIDTH, D_MODEL), const, pipeline_mode=once),
            pl.BlockSpec((SSM_WIDTH, D_MODEL), const, pipeline_mode=once),
            pl.BlockSpec((D_MODEL, D_MODEL), const, pipeline_mode=once),
            pl.BlockSpec((1, D_MODEL), const),
        ],
        out_specs=pl.BlockSpec((tm, D_MODEL), row),
        out_shape=jax.ShapeDtypeStruct((n, D_MODEL), F32),
        compiler_params=_cparams(("parallel",)),
        name="merge",
    )(x, o_att, y_raw, s_in, gate, d, wglu, bglu, pa, pb, wo, gpost)


def _to_slabs(s, nseq, nblk):
    t = s.reshape(nseq, nblk, SSM_LB, N_GB, LANES)
    return t.transpose(3, 1, 0, 2, 4).reshape(N_GB, nblk * nseq, SSM_K)


def _from_slabs(y, nseq, nblk):
    t = y.reshape(N_GB, nblk, nseq, SSM_LB, LANES)
    return t.transpose(2, 1, 3, 0, 4).reshape(nseq * nblk * SSM_LB, SSM_WIDTH)


def _state_to_slabs(h_re, h_im):
    nseq = h_re.shape[0]
    r = h_re.astype(F32).reshape(nseq, N_GB, SSM_S).transpose(1, 0, 2)
    i = h_im.astype(F32).reshape(nseq, N_GB, SSM_S).transpose(1, 0, 2)
    return jnp.concatenate([r, i], axis=-1)


def _state_from_slabs(h):
    nseq = h.shape[1]
    r = h[:, :, :SSM_S].transpose(1, 0, 2).reshape(nseq, N_GROUPS, STATE_DIM)
    i = h[:, :, SSM_S:].transpose(1, 0, 2).reshape(nseq, N_GROUPS, STATE_DIM)
    return r, i


def _pad_cols(w, n):
    return jnp.pad(w, ((0, 0), (0, 0), (0, n - w.shape[-1])))


def kernel(x_prompt, x_sample, cache_k, cache_v, cache_logf, state_ssm_re, state_ssm_im, norm_f1_pre, norm_f1_post, norm_mix_pre, norm_mix_post, norm_f2_pre, norm_f2_post, ffn1_w_gate, ffn1_w_up, ffn1_w_down, ffn2_w_gate, ffn2_w_up, ffn2_w_down, w_in, b_forget, b_gate, ssm_lam_re, ssm_lam_im, ssm_log_dt, ssm_b_re, ssm_b_im, ssm_c_re, ssm_c_im, ssm_d, ssm_w_glu, ssm_b_glu, w_proj_attn, w_proj_ssm, w_out):
    ffn_w = []
    for wg, wu, wd in ((ffn1_w_gate, ffn1_w_up, ffn1_w_down), (ffn2_w_gate, ffn2_w_up, ffn2_w_down)):
        ffn_w.append((_pad_cols(wg.astype(BF16), D_FF_PAD), _pad_cols(wu.astype(BF16), D_FF_PAD),
                      jnp.pad(wd.astype(BF16), ((0, 0), (0, D_FF_PAD - D_FF), (0, 0)))))
    a3 = 3 * ATT_WIDTH
    w_main = jnp.concatenate([w_in[:, :, :a3], w_in[:, :, a3 + N_HEADS:]], axis=-1).astype(BF16)
    w_f = _pad_cols(w_in[:, :, a3:a3 + N_HEADS].astype(BF16), LANES)
    b_f = _pad_cols(b_forget.astype(F32)[:, None, :], LANES)
    wglu = ssm_w_glu.astype(BF16)
    wpa = w_proj_attn.astype(BF16)
    wpb = w_proj_ssm.astype(BF16)
    wo = w_out.astype(BF16)
    wst, win, wca, a_lb = _ssm_weights(ssm_lam_re, ssm_lam_im, ssm_log_dt, ssm_b_re, ssm_b_im,
                                       ssm_c_re, ssm_c_im)
    row = lambda v: v.astype(F32)[:, None, :]
    g_f1a, g_f1b, g_ma, g_mb, g_f2a, g_f2b = map(row, (norm_f1_pre, norm_f1_post, norm_mix_pre,
                                                       norm_mix_post, norm_f2_pre, norm_f2_post))
    b_g, d_ssm, b_glu = row(b_gate), row(ssm_d), row(ssm_b_glu)

    x = jnp.concatenate([x_prompt.reshape(N_PROMPT, D_MODEL), x_sample.reshape(N_SAMPLE, D_MODEL)], axis=0)
    nblk_p = N_PROMPT // SSM_LB // SSM_SEG
    nblk_s = DEC_SEQ // SSM_LB
    h0_p = jnp.zeros((N_GB, 1, 2 * SSM_S), F32)
    lfc_all = cache_logf.astype(F32).transpose(0, 1, 3, 2)
    cache_k2 = cache_k.transpose(0, 1, 3, 4, 2)
    cache_v2 = cache_v.transpose(0, 1, 3, 4, 2)
    kp = jnp.zeros((DEPTH, N_PROMPT, ATT_WIDTH), F32)
    vp = jnp.zeros((DEPTH, N_PROMPT, ATT_WIDTH), F32)
    ks = jnp.zeros((DEPTH, N_SAMPLE, ATT_WIDTH), F32)
    vs = jnp.zeros((DEPTH, N_SAMPLE, ATT_WIDTH), F32)
    outs = [[] for _ in range(6)]
    for l in range(DEPTH):
        x = _ffn(x, g_f1a[l], g_f1b[l], *[w[l] for w in ffn_w[0]])
        q, s_in, gate, lf, kp, vp, ks, vs = _inproj(
            l, x, g_ma[l], w_main[l], w_f[l], b_f[l], b_g[l], kp, vp, ks, vs)
        logf = lf[:, :N_HEADS]
        logf_p = logf[:N_PROMPT]
        logf_s = logf[N_PROMPT:].reshape(DEC_BATCH, DEC_SEQ, N_HEADS)

        o_p = _attn_prompt(l, q, kp, vp, logf_p.T.reshape(N_HEADS // 2, 2, N_PROMPT))
        lfn = jnp.pad(logf_s.transpose(0, 2, 1), ((0, 0), (0, 0), (0, LANES - DEC_SEQ)))
        o_s = _attn_sample(l, q, ks, vs, cache_k2, cache_v2, lfc_all[l], lfn)
        o_att = jnp.concatenate([o_p, o_s], axis=0)

        u = s_in.astype(BF16)
        y_p, h_p = _ssm(_to_slabs(u[:N_PROMPT], SSM_SEG, nblk_p), wst[l], win[l], wca[l], a_lb[l],
                        h0_p, SSM_SEG, nblk_p, True)
        h0_s = _state_to_slabs(state_ssm_re[l], state_ssm_im[l])
        y_s, h_s = _ssm(_to_slabs(u[N_PROMPT:], DEC_BATCH, nblk_s), wst[l], win[l], wca[l], a_lb[l],
                        h0_s, DEC_BATCH, nblk_s, False)
        y_raw = jnp.concatenate([_from_slabs(y_p, SSM_SEG, nblk_p), _from_slabs(y_s, DEC_BATCH, nblk_s)], axis=0)

        x = _merge(x, o_att, y_raw, s_in, gate, d_ssm[l], wglu[l], b_glu[l], wpa[l], wpb[l], wo[l], g_mb[l])
        x = _ffn(x, g_f2a[l], g_f2b[l], *[w[l] for w in ffn_w[1]])

        hp_re, hp_im = _state_from_slabs(h_p)
        hs_re, hs_im = _state_from_slabs(h_s)
        for o, v in zip(outs, (logf_p.reshape(1, SEQ, N_HEADS), hp_re, hp_im, logf_s, hs_re, hs_im)):
            o.append(v)
    lf_p, hr_p, hi_p, lf_s, hr_s, hi_s = [jnp.stack(o) for o in outs]
    return (x[:N_PROMPT].reshape(1, SEQ, D_MODEL), x[N_PROMPT:].reshape(DEC_BATCH, DEC_SEQ, D_MODEL),
            kp.reshape(DEPTH, 1, SEQ, N_HEADS, HEAD_DIM), vp.reshape(DEPTH, 1, SEQ, N_HEADS, HEAD_DIM),
            lf_p, hr_p, hi_p,
            ks.reshape(DEPTH, DEC_BATCH, DEC_SEQ, N_HEADS, HEAD_DIM),
            vs.reshape(DEPTH, DEC_BATCH, DEC_SEQ, N_HEADS, HEAD_DIM),
            lf_s, hr_s, hi_s)
```

```python
import functools

import jax
import jax.numpy as jnp
from jax import lax
from jax.experimental import pallas as pl
from jax.experimental.pallas import tpu as pltpu

F32 = jnp.float32
BF16 = jnp.bfloat16

D_MODEL = 2048
SEQ = 8192
DEPTH = 4
DEC_BATCH = 32
DEC_SEQ = 32
PAST_LEN = 1024
N_HEADS = 16
HEAD_DIM = 64
ATT_WIDTH = N_HEADS * HEAD_DIM
SSM_WIDTH = D_MODEL // 2
SSM_GROUP = 16
N_GROUPS = SSM_WIDTH // SSM_GROUP
STATE_DIM = 64
D_FF = 5504
EPS = 1e-6

N_PROMPT = SEQ
N_SAMPLE = DEC_BATCH * DEC_SEQ
N_TOK = N_PROMPT + N_SAMPLE

LANES = 128
D_FF_PAD = 5632
FFN_TM = 512
FFN_TF = 512
INPROJ_TM = 512
INPROJ_TN = 1024
ATT_T = 256
ATT_QH = 2
MERGE_TM = 256
SSM_LB = 8
SSM_GB = 8
N_GB = N_GROUPS // SSM_GB
SSM_K = SSM_LB * LANES
SSM_S = SSM_GB * STATE_DIM
SSM_SEG = 8
NEG_BIG = -1e30
SKIP_GAP = 110.0
NORM_SLACK = 2.0 * 1.01
VMEM_LIMIT = 56 * 1024 * 1024


def _cparams(sem):
    return pltpu.CompilerParams(dimension_semantics=sem, vmem_limit_bytes=VMEM_LIMIT)


def _rms(x, g):
    ms = jnp.mean(x * x, axis=-1, keepdims=True)
    return x * lax.rsqrt(ms + EPS) * g


FFN_PTILES = N_PROMPT // FFN_TM


def _ffn_kernel(split_in, split_out, *refs):
    n_x = 2 if split_in else 1
    x_refs, (gpre_ref, gpost_ref, wg_ref, wu_ref, wd_ref), refs = refs[:n_x], refs[n_x:n_x + 5], refs[n_x + 5:]
    if split_out:
        o_refs, xn_ref, acc_ref = refs[:2], refs[2], refs[3]
    else:
        o_refs, xn_ref, acc_ref = refs[:1], refs[1], refs[0]
    i = pl.program_id(0)
    f = pl.program_id(1)
    nf = pl.num_programs(1)
    side = (i < FFN_PTILES, i >= FFN_PTILES)
    x_side = list(zip(side, x_refs)) if split_in else [(None, x_refs[0])]
    o_side = list(zip(side, o_refs)) if split_out else [(None, o_refs[0])]
    both = lambda a, b: a if b is None else a & b

    for cond, x_ref in x_side:
        @pl.when(both(f == 0, cond))
        def _(x_ref=x_ref):
            xn_ref[...] = _rms(x_ref[...], gpre_ref[...]).astype(BF16)
            acc_ref[...] = jnp.zeros(acc_ref.shape, F32)

    xn = xn_ref[...]
    g = jnp.dot(xn, wg_ref[...], preferred_element_type=F32)
    u = jnp.dot(xn, wu_ref[...], preferred_element_type=F32)
    h = (g * jax.nn.sigmoid(g) * u).astype(BF16)
    acc_ref[...] += jnp.dot(h, wd_ref[...], preferred_element_type=F32)

    for cx, x_ref in x_side:
        for co, o_ref in o_side:
            @pl.when(both(both(f == nf - 1, cx), co))
            def _(x_ref=x_ref, o_ref=o_ref):
                o_ref[...] = x_ref[...] + 0.5 * _rms(acc_ref[...], gpost_ref[...])


def _ffn(x, gpre, gpost, wg, wu, wd, split_out=False):
    split_in = isinstance(x, tuple)
    xs = x if split_in else (x,)
    grid = (N_TOK // FFN_TM, D_FF_PAD // FFN_TF)
    tile = (FFN_TM, D_MODEL)
    whole = [pl.BlockSpec(tile, lambda i, f: (i, 0))]
    halves = [pl.BlockSpec(tile, lambda i, f: (jnp.minimum(i, FFN_PTILES - 1), 0)),
              pl.BlockSpec(tile, lambda i, f: (jnp.maximum(i - FFN_PTILES, 0), 0))]
    sds = lambda n: jax.ShapeDtypeStruct((n, D_MODEL), F32)
    scratch = [pltpu.VMEM(tile, BF16)] + ([pltpu.VMEM(tile, F32)] if split_out else [])
    out = pl.pallas_call(
        functools.partial(_ffn_kernel, split_in, split_out),
        grid=grid,
        in_specs=(halves if split_in else whole) + [
            pl.BlockSpec((1, D_MODEL), lambda i, f: (0, 0)),
            pl.BlockSpec((1, D_MODEL), lambda i, f: (0, 0)),
            pl.BlockSpec((D_MODEL, FFN_TF), lambda i, f: (0, f)),
            pl.BlockSpec((D_MODEL, FFN_TF), lambda i, f: (0, f)),
            pl.BlockSpec((FFN_TF, D_MODEL), lambda i, f: (f, 0)),
        ],
        out_specs=halves if split_out else whole[0],
        out_shape=(sds(N_PROMPT), sds(N_SAMPLE)) if split_out else sds(N_TOK),
        scratch_shapes=scratch,
        compiler_params=_cparams(("arbitrary" if split_in or split_out else "parallel", "arbitrary")),
        name="ffn",
    )(*xs, gpre, gpost, wg, wu, wd)
    return tuple(out) if split_out else out


def _cast_pad_kernel(w_ref, o_ref):
    o_ref[0, :, :D_FF] = w_ref[0].astype(BF16)
    o_ref[0, :, D_FF:] = jnp.zeros((o_ref.shape[1], D_FF_PAD - D_FF), BF16)


def _cast_pad_cols(w):
    nl, r, c = w.shape
    tr = 256
    return pl.pallas_call(
        _cast_pad_kernel,
        grid=(nl, r // tr),
        in_specs=[pl.BlockSpec((1, tr, c), lambda l, i: (l, i, 0))],
        out_specs=pl.BlockSpec((1, tr, D_FF_PAD), lambda l, i: (l, i, 0)),
        out_shape=jax.ShapeDtypeStruct((nl, r, D_FF_PAD), BF16),
        compiler_params=_cparams(("parallel", "parallel")),
        name="cast_pad",
    )(w)


def _log_sigmoid(x):
    return jnp.minimum(x, 0.0) - jnp.log1p(jnp.exp(-jnp.abs(x)))


N_PTILES = N_PROMPT // INPROJ_TM


def _inproj_kernel(x_ref, g_ref, w_ref, wf_ref, bf_ref, bg_ref, kp_in, vp_in, ks_in, vs_in,
                   q_ref, s_ref, gate_ref, lf_ref, kp_ref, vp_ref, ks_ref, vs_ref,
                   xn_ref):
    del kp_in, vp_in, ks_in, vs_in
    i = pl.program_id(0)
    j = pl.program_id(1)

    @pl.when(j == 0)
    def _():
        xn = _rms(x_ref[...], g_ref[...]).astype(BF16)
        xn_ref[...] = xn
        fl = jnp.dot(xn, wf_ref[...], preferred_element_type=F32) + bf_ref[...]
        lf_ref[...] = _log_sigmoid(fl)

    z = jnp.dot(xn_ref[...], w_ref[...], preferred_element_type=F32)

    @pl.when(j == 0)
    def _():
        q_ref[...] = (z * (HEAD_DIM ** -0.5)).astype(BF16)

    for jj, (p_ref, s_out) in ((1, (kp_ref, ks_ref)), (2, (vp_ref, vs_ref))):
        @pl.when((j == jj) & (i < N_PTILES))
        def _(p_ref=p_ref):
            p_ref[0] = z

        @pl.when((j == jj) & (i >= N_PTILES))
        def _(s_out=s_out):
            s_out[0] = z

    @pl.when(j == 3)
    def _():
        s_ref[...] = z

    for jj in range(2 * D_MODEL // INPROJ_TN):
        @pl.when(j == 4 + jj)
        def _(jj=jj):
            sl = slice(jj * INPROJ_TN, (jj + 1) * INPROJ_TN)
            gate_ref[:, sl] = jax.nn.sigmoid(z + bg_ref[:, sl]).astype(BF16)


def _inproj(layer, x, g, w, wf, bf, bg, kp, vp, ks, vs):
    n = x.shape[0]
    tm = INPROJ_TM
    ncol = w.shape[1] // INPROJ_TN
    row = lambda i, j: (i, 0)
    const = lambda i, j: (0, 0)
    p_map = lambda i, j: (layer, jnp.minimum(i, N_PTILES - 1), 0)
    s_map = lambda i, j: (layer, jnp.maximum(i - N_PTILES, 0), 0)
    tok_outs = (
        jax.ShapeDtypeStruct((n, ATT_WIDTH), BF16),
        jax.ShapeDtypeStruct((n, SSM_WIDTH), F32),
        jax.ShapeDtypeStruct((n, 2 * D_MODEL), BF16),
        jax.ShapeDtypeStruct((n, LANES), F32),
    )
    kv_outs = tuple(jax.ShapeDtypeStruct(a.shape, a.dtype) for a in (kp, vp, ks, vs))
    any_spec = pl.BlockSpec(memory_space=pl.ANY)
    return pl.pallas_call(
        _inproj_kernel,
        grid=(n // tm, ncol),
        in_specs=[
            pl.BlockSpec((tm, D_MODEL), row),
            pl.BlockSpec((1, D_MODEL), const),
            pl.BlockSpec((D_MODEL, INPROJ_TN), lambda i, j: (0, j)),
            pl.BlockSpec((D_MODEL, LANES), const),
            pl.BlockSpec((1, LANES), const),
            pl.BlockSpec((1, 2 * D_MODEL), const),
            any_spec, any_spec, any_spec, any_spec,
        ],
        out_specs=[pl.BlockSpec((tm, o.shape[1]), row) for o in tok_outs] + [
            pl.BlockSpec((1, tm, ATT_WIDTH), p_map), pl.BlockSpec((1, tm, ATT_WIDTH), p_map),
            pl.BlockSpec((1, tm, ATT_WIDTH), s_map), pl.BlockSpec((1, tm, ATT_WIDTH), s_map),
        ],
        out_shape=tok_outs + kv_outs,
        input_output_aliases={6: 4, 7: 5, 8: 6, 9: 7},
        scratch_shapes=[pltpu.VMEM((tm, D_MODEL), BF16)],
        compiler_params=_cparams(("arbitrary", "arbitrary")),
        name="inproj",
    )(x, g, w, wf, bf, bg, kp, vp, ks, vs)


def _lane_cumsum(x, reverse=False):
    n = x.shape[-1]
    pos = lax.broadcasted_iota(jnp.int32, x.shape, x.ndim - 1)
    sh = 1
    while sh < n:
        if reverse:
            x = x + jnp.where(pos < n - sh, pltpu.roll(x, n - sh, axis=x.ndim - 1), 0.0)
        else:
            x = x + jnp.where(pos >= sh, pltpu.roll(x, sh, axis=x.ndim - 1), 0.0)
        sh *= 2
    return x


def _head_sq_norm_max(x, first):
    sq = x * x
    a = jnp.max(jnp.sum(jnp.where(first, sq, 0.0), axis=-1, keepdims=True), axis=0, keepdims=True)
    b = jnp.max(jnp.sum(jnp.where(first, 0.0, sq), axis=-1, keepdims=True), axis=0, keepdims=True)
    return jnp.concatenate([a, b], axis=0)


def _attn_prompt_kernel(q_ref, kf_ref, vf_ref, lft_ref, o_ref,
                        c_ref, ct_ref, kn_ref, k_ref, va_ref, vb_ref, m_ref, acc_ref, s_ref):
    i = pl.program_id(1)
    t = ATT_T
    s_len = k_ref.shape[0]
    nt = s_len // t
    lane = lax.broadcasted_iota(jnp.int32, (t, LANES), 1)
    first = lane < HEAD_DIM

    @pl.when(i == 0)
    def _():
        c = _lane_cumsum(lft_ref[0])
        c_ref[...] = c
        for jj in range(nt):
            ct_ref[jj] = c[:, jj * t:(jj + 1) * t]

        def prep(r, kn):
            rows = pl.ds(pl.multiple_of(r * t, t), t)
            kb = kf_ref[0, rows, :].astype(BF16)
            k_ref[rows, :] = kb
            kn = jnp.maximum(kn, _head_sq_norm_max(kb.astype(F32), first))
            v = vf_ref[0, rows, :]
            va_ref[rows, :] = jnp.where(first, v, jnp.where(lane == HEAD_DIM, 1.0, 0.0)).astype(BF16)
            vb_ref[rows, :] = jnp.where(first, jnp.where(lane == 0, 1.0, 0.0), v).astype(BF16)
            return kn

        kn_ref[...] = jnp.sqrt(lax.fori_loop(0, nt, prep, jnp.zeros((2, 1), F32)))

    nh = ATT_QH
    all_chains = list(range(2 * nh))
    q = q_ref[...]
    zero = jnp.zeros((t, LANES), q.dtype)
    qc = []
    for a in range(nh):
        qa = q[a * t:(a + 1) * t]
        qc += [jnp.where(first, qa, zero), jnp.where(first, zero, qa)]

    first_q = lax.broadcasted_iota(jnp.int32, q.shape, 1) < HEAD_DIM
    qn = jnp.sqrt(_head_sq_norm_max(q.astype(F32), first_q))
    c = c_ref[...]
    pos = lax.broadcasted_iota(jnp.int32, c.shape, 1)
    base = i * nh
    q0 = base * t
    c_first = jnp.max(jnp.where(pos >= q0, c, -jnp.inf), axis=-1, keepdims=True)
    thr = c_first + NORM_SLACK * qn * kn_ref[...] + SKIP_GAP
    need = jnp.where(pos < q0, jnp.where(c <= thr, 1.0, 0.0), 0.0)
    n_keys = jnp.max(jnp.sum(need, axis=-1, keepdims=True)).astype(jnp.int32)
    n_prev = lax.div(n_keys + (t - 1), t)

    m_ref[...] = jnp.full(m_ref.shape, NEG_BIG, F32)
    acc_ref[...] = jnp.zeros(acc_ref.shape, F32)
    keep = lax.broadcasted_iota(jnp.int32, (t, t), 1) <= lax.broadcasted_iota(jnp.int32, (t, t), 0)
    v_refs = (va_ref, vb_ref)

    def scores(j, chains, masked=()):
        k = k_ref[pl.ds(pl.multiple_of(j * t, t), t), :]
        ck = ct_ref[j]
        s = {}
        for ch in chains:
            h = ch % 2
            s_ch = lax.dot_general(qc[ch], k, (((1,), (1,)), ((), ())),
                                   preferred_element_type=F32) - ck[h:h + 1, :]
            s[ch] = jnp.where(keep, s_ch, NEG_BIG) if ch in masked else s_ch
        return s

    def accumulate(j, chains):
        rows = pl.ds(pl.multiple_of(j * t, t), t)
        s = {ch: s_ref[ch] for ch in chains}
        m_prev = {ch: m_ref[ch] for ch in chains}
        m_new = {ch: jnp.maximum(m_prev[ch], jnp.max(s[ch], axis=-1, keepdims=True)) for ch in chains}
        p = {ch: jnp.exp(s[ch] - jnp.tile(m_new[ch], (1, t // LANES))).astype(BF16) for ch in chains}
        alpha = {ch: jnp.exp(m_prev[ch] - m_new[ch]) for ch in chains}
        pv = {ch: jnp.dot(p[ch], v_refs[ch % 2][rows, :], preferred_element_type=F32) for ch in chains}
        for ch in chains:
            acc_ref[ch] = alpha[ch] * acc_ref[ch] + pv[ch]
            m_ref[ch] = m_new[ch]

    def park(s):
        for ch, s_ch in s.items():
            s_ref[ch] = s_ch

    own = [(b, [2 * a + h for a in range(b, nh) for h in range(2)], [2 * b, 2 * b + 1])
           for b in reversed(range(nh))]
    park(scores(base + own[0][0], own[0][1], own[0][2]))
    for (b_cur, ch_cur, _), (b_nxt, ch_nxt, mk_nxt) in zip(own[:-1], own[1:]):
        s_next = scores(base + b_nxt, ch_nxt, mk_nxt)
        accumulate(base + b_cur, ch_cur)
        park(s_next)

    def back(step, carry):
        s_next = scores(base - 1 - step, all_chains)
        accumulate(base - step, all_chains)
        park(s_next)
        return carry

    lax.fori_loop(0, n_prev, back, 0)
    accumulate(base - n_prev, all_chains)

    for a in range(nh):
        a0 = acc_ref[2 * a]
        a1 = acc_ref[2 * a + 1]
        o0 = a0 / a0[:, HEAD_DIM:HEAD_DIM + 1]
        o1 = a1 / a1[:, 0:1]
        o_ref[a * t:(a + 1) * t, :] = jnp.where(first, o0, o1).astype(o_ref.dtype)


def _attn_prompt(layer, q, kp, vp, lft):
    t = ATT_T
    nt = N_PROMPT // t
    tq = ATT_QH * t
    kv_spec = pl.BlockSpec((1, N_PROMPT, LANES), lambda j, i: (layer, 0, j))
    return pl.pallas_call(
        _attn_prompt_kernel,
        grid=(N_HEADS // 2, N_PROMPT // tq),
        in_specs=[
            pl.BlockSpec((tq, LANES), lambda j, i: (i, j)),
            kv_spec,
            kv_spec,
            pl.BlockSpec((1, 2, N_PROMPT), lambda j, i: (j, 0, 0)),
        ],
        out_specs=pl.BlockSpec((tq, LANES), lambda j, i: (i, j)),
        out_shape=jax.ShapeDtypeStruct((N_PROMPT, ATT_WIDTH), BF16),
        scratch_shapes=[
            pltpu.VMEM((2, N_PROMPT), F32),
            pltpu.VMEM((nt, 2, t), F32),
            pltpu.VMEM((2, 1), F32),
            pltpu.VMEM((N_PROMPT, LANES), BF16),
            pltpu.VMEM((N_PROMPT, LANES), BF16),
            pltpu.VMEM((N_PROMPT, LANES), BF16),
            pltpu.VMEM((2 * ATT_QH, t, LANES), F32),
            pltpu.VMEM((2 * ATT_QH, t, LANES), F32),
            pltpu.VMEM((2 * ATT_QH, t, t), F32),
        ],
        compiler_params=_cparams(("parallel", "arbitrary")),
        name="attn_prompt",
    )(q, kp, vp, lft)


SAMPLE_HEAD_GROUP = 4


def _attn_sample_kernel(q_ref, kn_ref, vn_ref, kc_ref, vc_ref, lfc_ref, lfn_ref, o_ref):
    t = q_ref.shape[0]
    q = q_ref[...]
    kn = kn_ref[0].astype(BF16)
    vn = vn_ref[0].astype(BF16)
    xc = lfc_ref[0]
    ckc = xc - _lane_cumsum(xc, reverse=True)
    ckn = _lane_cumsum(lfn_ref[0])[:, :t]
    keep = lax.broadcasted_iota(jnp.int32, (t, t), 1) <= lax.broadcasted_iota(jnp.int32, (t, t), 0)
    dn = (((1,), (1,)), ((), ()))
    outs = []
    for h0 in range(0, N_HEADS, SAMPLE_HEAD_GROUP):
        hs = range(h0, h0 + SAMPLE_HEAD_GROUP)
        sl = [slice(h * HEAD_DIM, (h + 1) * HEAD_DIM) for h in hs]
        kc = [kc_ref[0, 0, h].astype(BF16) for h in hs]
        vc = [vc_ref[0, 0, h].astype(BF16) for h in hs]
        s_c = [jnp.dot(q[:, s], k, preferred_element_type=F32) - ckc[h:h + 1, :]
               for h, s, k in zip(hs, sl, kc)]
        s_n = [jnp.where(keep, lax.dot_general(q[:, s], kn[:, s], dn, preferred_element_type=F32)
                         - ckn[h:h + 1, :], NEG_BIG) for h, s in zip(hs, sl)]
        m = [jnp.maximum(jnp.max(a, axis=-1, keepdims=True), jnp.max(b, axis=-1, keepdims=True))
             for a, b in zip(s_c, s_n)]
        p_c = [jnp.exp(a - mm) for a, mm in zip(s_c, m)]
        p_n = [jnp.exp(b - mm) for b, mm in zip(s_n, m)]
        l = [jnp.sum(a, axis=-1, keepdims=True) + jnp.sum(b, axis=-1, keepdims=True)
             for a, b in zip(p_c, p_n)]
        outs += [(lax.dot_general(a.astype(BF16), v, dn, preferred_element_type=F32)
                  + jnp.dot(b.astype(BF16), vn[:, s], preferred_element_type=F32)) / ll
                 for a, b, v, s, ll in zip(p_c, p_n, vc, sl, l)]
    o_ref[...] = jnp.concatenate(outs, axis=-1).astype(o_ref.dtype)


def _attn_sample(layer, q, ks, vs, cache_k, cache_v, lfc, lfn):
    t = DEC_SEQ
    new = lambda b: (layer, b, 0)
    cache_spec = pl.BlockSpec((1, 1, N_HEADS, HEAD_DIM, PAST_LEN), lambda b: (layer, b, 0, 0, 0))
    return pl.pallas_call(
        _attn_sample_kernel,
        grid=(DEC_BATCH,),
        in_specs=[
            pl.BlockSpec((t, ATT_WIDTH), lambda b: (N_PROMPT // t + b, 0)),
            pl.BlockSpec((1, t, ATT_WIDTH), new),
            pl.BlockSpec((1, t, ATT_WIDTH), new),
            cache_spec,
            cache_spec,
            pl.BlockSpec((1, N_HEADS, PAST_LEN), lambda b: (b, 0, 0)),
            pl.BlockSpec((1, N_HEADS, LANES), lambda b: (b, 0, 0)),
        ],
        out_specs=pl.BlockSpec((t, ATT_WIDTH), lambda b: (b, 0)),
        out_shape=jax.ShapeDtypeStruct((N_SAMPLE, ATT_WIDTH), BF16),
        compiler_params=_cparams(("parallel",)),
        name="attn_sample",
    )(q, ks, vs, cache_k, cache_v, lfc, lfn)


def _ssm_kernel(nseq, nblk, chained, u_ref, wst_ref, win_ref, wca_ref, a_ref, h0_ref, y_ref, hf_ref, sx_ref):
    nlb = 2 * SSM_S // LANES
    lanes = lambda x: [x[:, c * LANES:(c + 1) * LANES] for c in range(nlb)]
    u = u_ref[0]
    for c, blk in enumerate(lanes(jnp.dot(u, wst_ref[0], preferred_element_type=F32))):
        sx_ref[c] = blk
    a = lanes(a_ref[0])
    h0 = lanes(h0_ref[0])
    half = nlb // 2

    def step(a_, x):
        return ([a_[c] * x[c] - a_[c + half] * x[c + half] for c in range(half)]
                + [a_[c] * x[c + half] + a_[c + half] * x[c] for c in range(half)])

    def swap_in(rows, state):
        blk = [sx_ref[c, rows, :] for c in range(nlb)]
        for c in range(nlb):
            sx_ref[c, rows, :] = state[c]
        return blk

    rows_of = lambda j: pl.ds(pl.multiple_of(j * nseq, nseq), nseq)

    def body(j, state):
        blk = swap_in(rows_of(j), state)
        return [n + b for n, b in zip(step(a, state), blk)]

    if chained:
        ends = lax.fori_loop(0, nblk, body, [jnp.zeros((nseq, LANES), F32)] * nlb)
        a_seg = a
        for _ in range(nblk.bit_length() - 1):
            a_seg = step(a_seg, a_seg)
        state, starts = h0, []
        for s in range(nseq):
            starts.append(state)
            state = [n + e[s:s + 1] for n, e in zip(step(a_seg, state), ends)]
        final = state

        def fix(j, corr):
            for c in range(nlb):
                sx_ref[c, rows_of(j), :] = sx_ref[c, rows_of(j), :] + corr[c]
            return step(a, corr)

        lax.fori_loop(0, nblk, fix, [jnp.concatenate([st[c] for st in starts], axis=0) for c in range(nlb)])
    else:
        final = lax.fori_loop(0, nblk, body, h0)
    hf_ref[0] = jnp.concatenate(final, axis=-1)
    s_excl = jnp.concatenate([sx_ref[c] for c in range(nlb)], axis=-1).astype(BF16)
    y_ref[0] = (jnp.dot(u, win_ref[0], preferred_element_type=F32)
                + jnp.dot(s_excl, wca_ref[0], preferred_element_type=F32))


def _ssm(u, wst, win, wca, a, h0, nseq, nblk, chained):
    r = nblk * nseq
    nstate = 1 if chained else nseq
    if chained:
        assert nblk & (nblk - 1) == 0, nblk
    slab = lambda g: (g, 0, 0)
    return pl.pallas_call(
        functools.partial(_ssm_kernel, nseq, nblk, chained),
        grid=(N_GB,),
        in_specs=[
            pl.BlockSpec((1, r, SSM_K), slab),
            pl.BlockSpec((1, SSM_K, 2 * SSM_S), slab),
            pl.BlockSpec((1, SSM_K, SSM_K), slab),
            pl.BlockSpec((1, 2 * SSM_S, SSM_K), slab),
            pl.BlockSpec((1, 1, 2 * SSM_S), slab),
            pl.BlockSpec((1, nstate, 2 * SSM_S), slab),
        ],
        out_specs=[
            pl.BlockSpec((1, r, SSM_K), slab),
            pl.BlockSpec((1, nstate, 2 * SSM_S), slab),
        ],
        out_shape=(
            jax.ShapeDtypeStruct((N_GB, r, SSM_K), F32),
            jax.ShapeDtypeStruct((N_GB, nstate, 2 * SSM_S), F32),
        ),
        scratch_shapes=[pltpu.VMEM((2 * SSM_S // LANES, r, LANES), F32)],
        compiler_params=_cparams(("parallel",)),
        name="ssm_p" if chained else "ssm_s",
    )(u, wst, win, wca, a, h0)


def _ssm_weights(lam_re, lam_im, log_dt, b_re, b_im, c_re, c_im):
    hp = lax.Precision.HIGHEST
    nl = lam_re.shape[0]
    lb = SSM_LB
    dt = jnp.exp(log_dt.astype(F32))[..., None]
    lr, li = lam_re.astype(F32), lam_im.astype(F32)
    mag = jnp.exp(lr * dt)
    a_re = mag * jnp.cos(li * dt)
    a_im = mag * jnp.sin(li * dt)
    nr, ni = a_re - 1.0, a_im
    den = lr * lr + li * li
    f_re = (nr * lr + ni * li) / den
    f_im = (ni * lr - nr * li) / den
    br, bi = b_re.astype(F32), b_im.astype(F32)
    bb_re = f_re[..., None] * br - f_im[..., None] * bi
    bb_im = f_re[..., None] * bi + f_im[..., None] * br
    pr, pi = [jnp.ones_like(a_re)], [jnp.zeros_like(a_re)]
    for _ in range(lb):
        pr, pi = pr + [pr[-1] * a_re - pi[-1] * a_im], pi + [pr[-1] * a_im + pi[-1] * a_re]
    p_re, p_im = jnp.stack(pr), jnp.stack(pi)
    xb_re = p_re[:lb, ..., None] * bb_re - p_im[:lb, ..., None] * bb_im
    xb_im = p_re[:lb, ..., None] * bb_im + p_im[:lb, ..., None] * bb_re
    cr, ci = c_re.astype(F32), c_im.astype(F32)
    kt = (jnp.einsum('lgmp,tlgpn->tlgmn', cr, xb_re, precision=hp)
          - jnp.einsum('lgmp,tlgpn->tlgmn', ci, xb_im, precision=hp))
    ktc = kt.reshape(lb, nl, N_GB, SSM_GB, SSM_GROUP, SSM_GROUP).transpose(1, 2, 0, 3, 5, 4)
    xs = jnp.stack([xb_re[::-1], xb_im[::-1]], axis=1)
    xsc = xs.reshape(lb, 2, nl, N_GB, SSM_GB, STATE_DIM, SSM_GROUP).transpose(2, 3, 0, 1, 4, 6, 5)
    m_re = cr[None] * p_re[1:, :, :, None, :] - ci[None] * p_im[1:, :, :, None, :]
    m_im = cr[None] * p_im[1:, :, :, None, :] + ci[None] * p_re[1:, :, :, None, :]
    ms = jnp.stack([m_re, -m_im], axis=0)
    mcc = ms.reshape(2, lb, nl, N_GB, SSM_GB, SSM_GROUP, STATE_DIM).transpose(2, 3, 0, 1, 4, 6, 5)
    win, wst, wca = _ssm_expand(ktc, xsc, mcc)
    a_lb = jnp.concatenate([p_re[lb].reshape(nl, N_GB, 1, SSM_S),
                            p_im[lb].reshape(nl, N_GB, 1, SSM_S)], axis=-1)
    return wst, win, wca, a_lb


def _ssm_expand_kernel(kt_ref, xs_ref, mc_ref, win_ref, wst_ref, wca_ref, w_ref, bd_ref):
    lb, gb, grp, sd = SSM_LB, SSM_GB, SSM_GROUP, STATE_DIM
    bd_ref[...] = jnp.zeros(bd_ref.shape, F32)
    for tau in range(lb):
        for g in range(gb):
            bd_ref[tau, g * grp:(g + 1) * grp, g * grp:(g + 1) * grp] = kt_ref[0, 0, tau, g]
    w_ref[...] = jnp.zeros(w_ref.shape, F32)
    for r0 in range(lb):
        for r1 in range(r0, lb):
            w_ref[r0 * LANES:(r0 + 1) * LANES, r1 * LANES:(r1 + 1) * LANES] = bd_ref[r1 - r0]
    win_ref[0, 0] = w_ref[...].astype(BF16)

    w_ref[...] = jnp.zeros(w_ref.shape, F32)
    for r0 in range(lb):
        for ri in range(2):
            for g in range(gb):
                rows = slice(r0 * LANES + g * grp, r0 * LANES + (g + 1) * grp)
                cols = slice(ri * SSM_S + g * sd, ri * SSM_S + (g + 1) * sd)
                w_ref[rows, cols] = xs_ref[0, 0, r0, ri, g]
    wst_ref[0, 0] = w_ref[...].astype(BF16)

    w_ref[...] = jnp.zeros(w_ref.shape, F32)
    for ri in range(2):
        for r1 in range(lb):
            for g in range(gb):
                rows = slice(ri * SSM_S + g * sd, ri * SSM_S + (g + 1) * sd)
                cols = slice(r1 * LANES + g * grp, r1 * LANES + (g + 1) * grp)
                w_ref[rows, cols] = mc_ref[0, 0, ri, r1, g]
    wca_ref[0, 0] = w_ref[...].astype(BF16)


def _ssm_expand(ktc, xsc, mcc):
    nl = ktc.shape[0]
    lb, gb, grp, sd = SSM_LB, SSM_GB, SSM_GROUP, STATE_DIM
    slab = lambda l, b: (l, b, 0, 0)
    out = jax.ShapeDtypeStruct((nl, N_GB, SSM_K, SSM_K), BF16)
    return pl.pallas_call(
        _ssm_expand_kernel,
        grid=(nl, N_GB),
        in_specs=[
            pl.BlockSpec((1, 1, lb, gb, grp, grp), lambda l, b: (l, b, 0, 0, 0, 0)),
            pl.BlockSpec((1, 1, lb, 2, gb, grp, sd), lambda l, b: (l, b, 0, 0, 0, 0, 0)),
            pl.BlockSpec((1, 1, 2, lb, gb, sd, grp), lambda l, b: (l, b, 0, 0, 0, 0, 0)),
        ],
        out_specs=[pl.BlockSpec((1, 1, SSM_K, SSM_K), slab)] * 3,
        out_shape=(out, out, out),
        scratch_shapes=[pltpu.VMEM((SSM_K, SSM_K), F32), pltpu.VMEM((lb, LANES, LANES), F32)],
        compiler_params=_cparams(("parallel", "parallel")),
        name="ssm_expand",
    )(ktc, xsc, mcc)


def _merge_kernel(x_ref, oatt_ref, y_ref, s_ref, gate_ref, d_ref, wglu_ref, bglu_ref,
                  pa_ref, pb_ref, wo_ref, gpost_ref, o_ref):
    y = jax.nn.gelu(y_ref[...] + d_ref[...] * s_ref[...])
    glu = jnp.dot(y.astype(BF16), wglu_ref[...], preferred_element_type=F32) + bglu_ref[...]
    o_ssm = (y * jax.nn.sigmoid(glu)).astype(BF16)
    pa = jnp.dot(oatt_ref[...], pa_ref[...], preferred_element_type=F32)
    pb = jnp.dot(o_ssm, pb_ref[...], preferred_element_type=F32)
    gate = gate_ref[...].astype(F32)
    merged = gate[:, :D_MODEL] * pa + gate[:, D_MODEL:] * pb
    m = jnp.dot(merged.astype(BF16), wo_ref[...], preferred_element_type=F32)
    o_ref[...] = x_ref[...] + _rms(m, gpost_ref[...])


def _merge(x, o_att, y_raw, s_in, gate, d, wglu, bglu, pa, pb, wo, gpost):
    n = x.shape[0]
    tm = MERGE_TM
    row = lambda i: (i, 0)
    const = lambda i: (0, 0)
    once = pl.Buffered(1)
    return pl.pallas_call(
        _merge_kernel,
        grid=(n // tm,),
        in_specs=[
            pl.BlockSpec((tm, D_MODEL), row),
            pl.BlockSpec((tm, ATT_WIDTH), row),
            pl.BlockSpec((tm, SSM_WIDTH), row),
            pl.BlockSpec((tm, SSM_WIDTH), row),
            pl.BlockSpec((tm, 2 * D_MODEL), row),
            pl.BlockSpec((1, SSM_WIDTH), const),
            pl.BlockSpec((SSM_WIDTH, SSM_WIDTH), const, pipeline_mode=once),
            pl.BlockSpec((1, SSM_WIDTH), const),
            pl.BlockSpec((ATT_WIDTH, D_MODEL), const, pipeline_mode=once),
            pl.BlockSpec((SSM_WIDTH, D_MODEL), const, pipeline_mode=once),
            pl.BlockSpec((D_MODEL, D_MODEL), const, pipeline_mode=once),
            pl.BlockSpec((1, D_MODEL), const),
        ],
        out_specs=pl.BlockSpec((tm, D_MODEL), row),
        out_shape=jax.ShapeDtypeStruct((n, D_MODEL), F32),
        compiler_params=_cparams(("parallel",)),
        name="merge",
    )(x, o_att, y_raw, s_in, gate, d, wglu, bglu, pa, pb, wo, gpost)


def _to_slabs(s, nseq, nblk):
    t = s.reshape(nseq, nblk, SSM_LB, N_GB, LANES)
    return t.transpose(3, 1, 0, 2, 4).reshape(N_GB, nblk * nseq, SSM_K)


def _from_slabs(y, nseq, nblk):
    t = y.reshape(N_GB, nblk, nseq, SSM_LB, LANES)
    return t.transpose(2, 1, 3, 0, 4).reshape(nseq * nblk * SSM_LB, SSM_WIDTH)


def _state_to_slabs(h_re, h_im):
    nseq = h_re.shape[0]
    r = h_re.astype(F32).reshape(nseq, N_GB, SSM_S).transpose(1, 0, 2)
    i = h_im.astype(F32).reshape(nseq, N_GB, SSM_S).transpose(1, 0, 2)
    return jnp.concatenate([r, i], axis=-1)


def _state_from_slabs(h):
    nseq = h.shape[1]
    r = h[:, :, :SSM_S].transpose(1, 0, 2).reshape(nseq, N_GROUPS, STATE_DIM)
    i = h[:, :, SSM_S:].transpose(1, 0, 2).reshape(nseq, N_GROUPS, STATE_DIM)
    return r, i


def _pad_cols(w, n):
    return jnp.pad(w, ((0, 0), (0, 0), (0, n - w.shape[-1])))


def kernel(x_prompt, x_sample, cache_k, cache_v, cache_logf, state_ssm_re, state_ssm_im, norm_f1_pre, norm_f1_post, norm_mix_pre, norm_mix_post, norm_f2_pre, norm_f2_post, ffn1_w_gate, ffn1_w_up, ffn1_w_down, ffn2_w_gate, ffn2_w_up, ffn2_w_down, w_in, b_forget, b_gate, ssm_lam_re, ssm_lam_im, ssm_log_dt, ssm_b_re, ssm_b_im, ssm_c_re, ssm_c_im, ssm_d, ssm_w_glu, ssm_b_glu, w_proj_attn, w_proj_ssm, w_out):
    ffn_w = []
    for wg, wu, wd in ((ffn1_w_gate, ffn1_w_up, ffn1_w_down), (ffn2_w_gate, ffn2_w_up, ffn2_w_down)):
        ffn_w.append((_cast_pad_cols(wg), _cast_pad_cols(wu),
                      jnp.pad(wd.astype(BF16), ((0, 0), (0, D_FF_PAD - D_FF), (0, 0)))))
    a3 = 3 * ATT_WIDTH
    w_main = jnp.concatenate([w_in[:, :, :a3], w_in[:, :, a3 + N_HEADS:]], axis=-1).astype(BF16)
    w_f = _pad_cols(w_in[:, :, a3:a3 + N_HEADS].astype(BF16), LANES)
    b_f = _pad_cols(b_forget.astype(F32)[:, None, :], LANES)
    wglu = ssm_w_glu.astype(BF16)
    wpa = w_proj_attn.astype(BF16)
    wpb = w_proj_ssm.astype(BF16)
    wo = w_out.astype(BF16)
    wst, win, wca, a_lb = _ssm_weights(ssm_lam_re, ssm_lam_im, ssm_log_dt, ssm_b_re, ssm_b_im,
                                       ssm_c_re, ssm_c_im)
    row = lambda v: v.astype(F32)[:, None, :]
    g_f1a, g_f1b, g_ma, g_mb, g_f2a, g_f2b = map(row, (norm_f1_pre, norm_f1_post, norm_mix_pre,
                                                       norm_mix_post, norm_f2_pre, norm_f2_post))
    b_g, d_ssm, b_glu = row(b_gate), row(ssm_d), row(ssm_b_glu)

    x = (x_prompt.reshape(N_PROMPT, D_MODEL), x_sample.reshape(N_SAMPLE, D_MODEL))
    nblk_p = N_PROMPT // SSM_LB // SSM_SEG
    nblk_s = DEC_SEQ // SSM_LB
    h0_p = jnp.zeros((N_GB, 1, 2 * SSM_S), F32)
    lfc_all = cache_logf.astype(F32).transpose(0, 1, 3, 2)
    cache_k2 = cache_k.transpose(0, 1, 3, 4, 2)
    cache_v2 = cache_v.transpose(0, 1, 3, 4, 2)
    kp = jnp.zeros((DEPTH, N_PROMPT, ATT_WIDTH), F32)
    vp = jnp.zeros((DEPTH, N_PROMPT, ATT_WIDTH), F32)
    ks = jnp.zeros((DEPTH, N_SAMPLE, ATT_WIDTH), F32)
    vs = jnp.zeros((DEPTH, N_SAMPLE, ATT_WIDTH), F32)
    outs = [[] for _ in range(6)]
    for l in range(DEPTH):
        x = _ffn(x, g_f1a[l], g_f1b[l], *[w[l] for w in ffn_w[0]])
        q, s_in, gate, lf, kp, vp, ks, vs = _inproj(
            l, x, g_ma[l], w_main[l], w_f[l], b_f[l], b_g[l], kp, vp, ks, vs)
        logf = lf[:, :N_HEADS]
        logf_p = logf[:N_PROMPT]
        logf_s = logf[N_PROMPT:].reshape(DEC_BATCH, DEC_SEQ, N_HEADS)

        o_p = _attn_prompt(l, q, kp, vp, logf_p.T.reshape(N_HEADS // 2, 2, N_PROMPT))
        lfn = jnp.pad(logf_s.transpose(0, 2, 1), ((0, 0), (0, 0), (0, LANES - DEC_SEQ)))
        o_s = _attn_sample(l, q, ks, vs, cache_k2, cache_v2, lfc_all[l], lfn)
        o_att = jnp.concatenate([o_p, o_s], axis=0)

        u = s_in.astype(BF16)
        y_p, h_p = _ssm(_to_slabs(u[:N_PROMPT], SSM_SEG, nblk_p), wst[l], win[l], wca[l], a_lb[l],
                        h0_p, SSM_SEG, nblk_p, True)
        h0_s = _state_to_slabs(state_ssm_re[l], state_ssm_im[l])
        y_s, h_s = _ssm(_to_slabs(u[N_PROMPT:], DEC_BATCH, nblk_s), wst[l], win[l], wca[l], a_lb[l],
                        h0_s, DEC_BATCH, nblk_s, False)
        y_raw = jnp.concatenate([_from_slabs(y_p, SSM_SEG, nblk_p), _from_slabs(y_s, DEC_BATCH, nblk_s)], axis=0)

        x = _merge(x, o_att, y_raw, s_in, gate, d_ssm[l], wglu[l], b_glu[l], wpa[l], wpb[l], wo[l], g_mb[l])
        x = _ffn(x, g_f2a[l], g_f2b[l], *[w[l] for w in ffn_w[1]], split_out=(l == DEPTH - 1))

        hp_re, hp_im = _state_from_slabs(h_p)
        hs_re, hs_im = _state_from_slabs(h_s)
        for o, v in zip(outs, (logf_p.reshape(1, SEQ, N_HEADS), hp_re, hp_im, logf_s, hs_re, hs_im)):
            o.append(v)
    lf_p, hr_p, hi_p, lf_s, hr_s, hi_s = [jnp.stack(o) for o in outs]
    return (x[0].reshape(1, SEQ, D_MODEL), x[1].reshape(DEC_BATCH, DEC_SEQ, D_MODEL),
            kp.reshape(DEPTH, 1, SEQ, N_HEADS, HEAD_DIM), vp.reshape(DEPTH, 1, SEQ, N_HEADS, HEAD_DIM),
            lf_p, hr_p, hi_p,
            ks.reshape(DEPTH, DEC_BATCH, DEC_SEQ, N_HEADS, HEAD_DIM),
            vs.reshape(DEPTH, DEC_BATCH, DEC_SEQ, N_HEADS, HEAD_DIM),
            lf_s, hr_s, hi_s)
```

```python
import functools

import jax
import jax.numpy as jnp
from jax import lax
from jax.experimental import pallas as pl
from jax.experimental.pallas import tpu as pltpu

F32 = jnp.float32
BF16 = jnp.bfloat16

D_MODEL = 2048
SEQ = 8192
DEPTH = 4
DEC_BATCH = 32
DEC_SEQ = 32
PAST_LEN = 1024
N_HEADS = 16
HEAD_DIM = 64
ATT_WIDTH = N_HEADS * HEAD_DIM
SSM_WIDTH = D_MODEL // 2
SSM_GROUP = 16
N_GROUPS = SSM_WIDTH // SSM_GROUP
STATE_DIM = 64
D_FF = 5504
EPS = 1e-6

N_PROMPT = SEQ
N_SAMPLE = DEC_BATCH * DEC_SEQ
N_TOK = N_PROMPT + N_SAMPLE

LANES = 128
D_FF_PAD = 5632
FFN_TM = 512
FFN_TF = 512
INPROJ_TM = 512
INPROJ_TN = 1024
ATT_T = 256
ATT_QH = 2
MERGE_TM = 256
SSM_LB = 8
SSM_GB = 8
N_GB = N_GROUPS // SSM_GB
SSM_K = SSM_LB * LANES
SSM_S = SSM_GB * STATE_DIM
SSM_SEG = 8
NEG_BIG = -1e30
LOG2E = 1.4426950408889634
SKIP_GAP = 110.0 * LOG2E
NORM_SLACK = 2.0 * 1.01
VMEM_LIMIT = 56 * 1024 * 1024


def _cparams(sem):
    return pltpu.CompilerParams(dimension_semantics=sem, vmem_limit_bytes=VMEM_LIMIT)


def _rms(x, g):
    ms = jnp.mean(x * x, axis=-1, keepdims=True)
    return x * lax.rsqrt(ms + EPS) * g


FFN_PTILES = N_PROMPT // FFN_TM


def _ffn_kernel(split_in, split_out, *refs):
    n_x = 2 if split_in else 1
    x_refs, (gpre_ref, gpost_ref, wg_ref, wu_ref, wd_ref), refs = refs[:n_x], refs[n_x:n_x + 5], refs[n_x + 5:]
    if split_out:
        o_refs, xn_ref, acc_ref = refs[:2], refs[2], refs[3]
    else:
        o_refs, xn_ref, acc_ref = refs[:1], refs[1], refs[0]
    i = pl.program_id(0)
    f = pl.program_id(1)
    nf = pl.num_programs(1)
    side = (i < FFN_PTILES, i >= FFN_PTILES)
    x_side = list(zip(side, x_refs)) if split_in else [(None, x_refs[0])]
    o_side = list(zip(side, o_refs)) if split_out else [(None, o_refs[0])]
    both = lambda a, b: a if b is None else a & b

    for cond, x_ref in x_side:
        @pl.when(both(f == 0, cond))
        def _(x_ref=x_ref):
            xn_ref[...] = _rms(x_ref[...], gpre_ref[...]).astype(BF16)
            acc_ref[...] = jnp.zeros(acc_ref.shape, F32)

    xn = xn_ref[...]
    g = jnp.dot(xn, wg_ref[...], preferred_element_type=F32)
    u = jnp.dot(xn, wu_ref[...], preferred_element_type=F32)
    h = (g * jax.nn.sigmoid(g) * u).astype(BF16)
    acc_ref[...] += jnp.dot(h, wd_ref[...], preferred_element_type=F32)

    for cx, x_ref in x_side:
        for co, o_ref in o_side:
            @pl.when(both(both(f == nf - 1, cx), co))
            def _(x_ref=x_ref, o_ref=o_ref):
                o_ref[...] = x_ref[...] + 0.5 * _rms(acc_ref[...], gpost_ref[...])


def _ffn(x, gpre, gpost, wg, wu, wd, split_out=False):
    split_in = isinstance(x, tuple)
    xs = x if split_in else (x,)
    grid = (N_TOK // FFN_TM, D_FF_PAD // FFN_TF)
    tile = (FFN_TM, D_MODEL)
    whole = [pl.BlockSpec(tile, lambda i, f: (i, 0))]
    halves = [pl.BlockSpec(tile, lambda i, f: (jnp.minimum(i, FFN_PTILES - 1), 0)),
              pl.BlockSpec(tile, lambda i, f: (jnp.maximum(i - FFN_PTILES, 0), 0))]
    sds = lambda n: jax.ShapeDtypeStruct((n, D_MODEL), F32)
    scratch = [pltpu.VMEM(tile, BF16)] + ([pltpu.VMEM(tile, F32)] if split_out else [])
    out = pl.pallas_call(
        functools.partial(_ffn_kernel, split_in, split_out),
        grid=grid,
        in_specs=(halves if split_in else whole) + [
            pl.BlockSpec((1, D_MODEL), lambda i, f: (0, 0)),
            pl.BlockSpec((1, D_MODEL), lambda i, f: (0, 0)),
            pl.BlockSpec((D_MODEL, FFN_TF), lambda i, f: (0, f)),
            pl.BlockSpec((D_MODEL, FFN_TF), lambda i, f: (0, f)),
            pl.BlockSpec((FFN_TF, D_MODEL), lambda i, f: (f, 0)),
        ],
        out_specs=halves if split_out else whole[0],
        out_shape=(sds(N_PROMPT), sds(N_SAMPLE)) if split_out else sds(N_TOK),
        scratch_shapes=scratch,
        compiler_params=_cparams(("arbitrary" if split_in or split_out else "parallel", "arbitrary")),
        name="ffn",
    )(*xs, gpre, gpost, wg, wu, wd)
    return tuple(out) if split_out else out


def _cast_pad_kernel(w_ref, o_ref):
    o_ref[0, :, :D_FF] = w_ref[0].astype(BF16)
    o_ref[0, :, D_FF:] = jnp.zeros((o_ref.shape[1], D_FF_PAD - D_FF), BF16)


def _cast_pad_cols(w):
    nl, r, c = w.shape
    tr = 256
    return pl.pallas_call(
        _cast_pad_kernel,
        grid=(nl, r // tr),
        in_specs=[pl.BlockSpec((1, tr, c), lambda l, i: (l, i, 0))],
        out_specs=pl.BlockSpec((1, tr, D_FF_PAD), lambda l, i: (l, i, 0)),
        out_shape=jax.ShapeDtypeStruct((nl, r, D_FF_PAD), BF16),
        compiler_params=_cparams(("parallel", "parallel")),
        name="cast_pad",
    )(w)


def _log_sigmoid(x):
    return jnp.minimum(x, 0.0) - jnp.log1p(jnp.exp(-jnp.abs(x)))


N_PTILES = N_PROMPT // INPROJ_TM


def _inproj_kernel(x_ref, g_ref, w_ref, wf_ref, bf_ref, bg_ref, kp_in, vp_in, ks_in, vs_in,
                   q_ref, s_ref, gate_ref, lf_ref, kp_ref, vp_ref, ks_ref, vs_ref,
                   xn_ref):
    del kp_in, vp_in, ks_in, vs_in
    i = pl.program_id(0)
    j = pl.program_id(1)

    @pl.when(j == 0)
    def _():
        xn = _rms(x_ref[...], g_ref[...]).astype(BF16)
        xn_ref[...] = xn
        fl = jnp.dot(xn, wf_ref[...], preferred_element_type=F32) + bf_ref[...]
        lf_ref[...] = _log_sigmoid(fl)

    z = jnp.dot(xn_ref[...], w_ref[...], preferred_element_type=F32)

    @pl.when(j == 0)
    def _():
        q_ref[...] = (z * (HEAD_DIM ** -0.5 * LOG2E)).astype(BF16)

    for jj, (p_ref, s_out) in ((1, (kp_ref, ks_ref)), (2, (vp_ref, vs_ref))):
        @pl.when((j == jj) & (i < N_PTILES))
        def _(p_ref=p_ref):
            p_ref[0] = z

        @pl.when((j == jj) & (i >= N_PTILES))
        def _(s_out=s_out):
            s_out[0] = z

    @pl.when(j == 3)
    def _():
        s_ref[...] = z

    for jj in range(2 * D_MODEL // INPROJ_TN):
        @pl.when(j == 4 + jj)
        def _(jj=jj):
            sl = slice(jj * INPROJ_TN, (jj + 1) * INPROJ_TN)
            gate_ref[:, sl] = jax.nn.sigmoid(z + bg_ref[:, sl]).astype(BF16)


def _inproj(layer, x, g, w, wf, bf, bg, kp, vp, ks, vs):
    n = x.shape[0]
    tm = INPROJ_TM
    ncol = w.shape[1] // INPROJ_TN
    row = lambda i, j: (i, 0)
    const = lambda i, j: (0, 0)
    p_map = lambda i, j: (layer, jnp.minimum(i, N_PTILES - 1), 0)
    s_map = lambda i, j: (layer, jnp.maximum(i - N_PTILES, 0), 0)
    tok_outs = (
        jax.ShapeDtypeStruct((n, ATT_WIDTH), BF16),
        jax.ShapeDtypeStruct((n, SSM_WIDTH), F32),
        jax.ShapeDtypeStruct((n, 2 * D_MODEL), BF16),
        jax.ShapeDtypeStruct((n, LANES), F32),
    )
    kv_outs = tuple(jax.ShapeDtypeStruct(a.shape, a.dtype) for a in (kp, vp, ks, vs))
    any_spec = pl.BlockSpec(memory_space=pl.ANY)
    return pl.pallas_call(
        _inproj_kernel,
        grid=(n // tm, ncol),
        in_specs=[
            pl.BlockSpec((tm, D_MODEL), row),
            pl.BlockSpec((1, D_MODEL), const),
            pl.BlockSpec((D_MODEL, INPROJ_TN), lambda i, j: (0, j)),
            pl.BlockSpec((D_MODEL, LANES), const),
            pl.BlockSpec((1, LANES), const),
            pl.BlockSpec((1, 2 * D_MODEL), const),
            any_spec, any_spec, any_spec, any_spec,
        ],
        out_specs=[pl.BlockSpec((tm, o.shape[1]), row) for o in tok_outs] + [
            pl.BlockSpec((1, tm, ATT_WIDTH), p_map), pl.BlockSpec((1, tm, ATT_WIDTH), p_map),
            pl.BlockSpec((1, tm, ATT_WIDTH), s_map), pl.BlockSpec((1, tm, ATT_WIDTH), s_map),
        ],
        out_shape=tok_outs + kv_outs,
        input_output_aliases={6: 4, 7: 5, 8: 6, 9: 7},
        scratch_shapes=[pltpu.VMEM((tm, D_MODEL), BF16)],
        compiler_params=_cparams(("arbitrary", "arbitrary")),
        name="inproj",
    )(x, g, w, wf, bf, bg, kp, vp, ks, vs)


def _lane_cumsum(x, reverse=False):
    n = x.shape[-1]
    pos = lax.broadcasted_iota(jnp.int32, x.shape, x.ndim - 1)
    sh = 1
    while sh < n:
        if reverse:
            x = x + jnp.where(pos < n - sh, pltpu.roll(x, n - sh, axis=x.ndim - 1), 0.0)
        else:
            x = x + jnp.where(pos >= sh, pltpu.roll(x, sh, axis=x.ndim - 1), 0.0)
        sh *= 2
    return x


def _head_sq_norm_max(x, first):
    sq = x * x
    a = jnp.max(jnp.sum(jnp.where(first, sq, 0.0), axis=-1, keepdims=True), axis=0, keepdims=True)
    b = jnp.max(jnp.sum(jnp.where(first, 0.0, sq), axis=-1, keepdims=True), axis=0, keepdims=True)
    return jnp.concatenate([a, b], axis=0)


def _attn_prompt_kernel(q_ref, kf_ref, vf_ref, lft_ref, o_ref,
                        c_ref, ct_ref, kn_ref, k_ref, va_ref, vb_ref, m_ref, acc_ref, s_ref):
    i = pl.program_id(1)
    t = ATT_T
    s_len = k_ref.shape[0]
    nt = s_len // t
    lane = lax.broadcasted_iota(jnp.int32, (t, LANES), 1)
    first = lane < HEAD_DIM

    @pl.when(i == 0)
    def _():
        c = _lane_cumsum(lft_ref[0]) * LOG2E
        c_ref[...] = c
        for jj in range(nt):
            ct_ref[jj] = c[:, jj * t:(jj + 1) * t]

        def prep(r, kn):
            rows = pl.ds(pl.multiple_of(r * t, t), t)
            kb = kf_ref[0, rows, :].astype(BF16)
            k_ref[rows, :] = kb
            kn = jnp.maximum(kn, _head_sq_norm_max(kb.astype(F32), first))
            v = vf_ref[0, rows, :]
            va_ref[rows, :] = jnp.where(first, v, jnp.where(lane == HEAD_DIM, 1.0, 0.0)).astype(BF16)
            vb_ref[rows, :] = jnp.where(first, jnp.where(lane == 0, 1.0, 0.0), v).astype(BF16)
            return kn

        kn_ref[...] = jnp.sqrt(lax.fori_loop(0, nt, prep, jnp.zeros((2, 1), F32)))

    nh = ATT_QH
    all_chains = list(range(2 * nh))
    q = q_ref[...]
    zero = jnp.zeros((t, LANES), q.dtype)
    qc = []
    for a in range(nh):
        qa = q[a * t:(a + 1) * t]
        qc += [jnp.where(first, qa, zero), jnp.where(first, zero, qa)]

    first_q = lax.broadcasted_iota(jnp.int32, q.shape, 1) < HEAD_DIM
    qn = jnp.sqrt(_head_sq_norm_max(q.astype(F32), first_q))
    c = c_ref[...]
    pos = lax.broadcasted_iota(jnp.int32, c.shape, 1)
    base = i * nh
    q0 = base * t
    c_first = jnp.max(jnp.where(pos >= q0, c, -jnp.inf), axis=-1, keepdims=True)
    thr = c_first + NORM_SLACK * qn * kn_ref[...] + SKIP_GAP
    need = jnp.where(pos < q0, jnp.where(c <= thr, 1.0, 0.0), 0.0)
    n_keys = jnp.max(jnp.sum(need, axis=-1, keepdims=True)).astype(jnp.int32)
    n_prev = lax.div(n_keys + (t - 1), t)

    m_ref[...] = jnp.full(m_ref.shape, NEG_BIG, F32)
    acc_ref[...] = jnp.zeros(acc_ref.shape, F32)
    keep = lax.broadcasted_iota(jnp.int32, (t, t), 1) <= lax.broadcasted_iota(jnp.int32, (t, t), 0)
    v_refs = (va_ref, vb_ref)

    def scores(j, chains, masked=()):
        k = k_ref[pl.ds(pl.multiple_of(j * t, t), t), :]
        ck = ct_ref[j]
        s = {}
        for ch in chains:
            h = ch % 2
            s_ch = lax.dot_general(qc[ch], k, (((1,), (1,)), ((), ())),
                                   preferred_element_type=F32) - ck[h:h + 1, :]
            s[ch] = jnp.where(keep, s_ch, NEG_BIG) if ch in masked else s_ch
        return s

    def accumulate(j, chains):
        rows = pl.ds(pl.multiple_of(j * t, t), t)
        s = {ch: s_ref[ch] for ch in chains}
        m_prev = {ch: m_ref[ch] for ch in chains}
        m_new = {ch: jnp.maximum(m_prev[ch], jnp.max(s[ch], axis=-1, keepdims=True)) for ch in chains}
        p = {ch: jnp.exp2(s[ch] - jnp.tile(m_new[ch], (1, t // LANES))).astype(BF16) for ch in chains}
        alpha = {ch: jnp.exp2(m_prev[ch] - m_new[ch]) for ch in chains}
        pv = {ch: jnp.dot(p[ch], v_refs[ch % 2][rows, :], preferred_element_type=F32) for ch in chains}
        for ch in chains:
            acc_ref[ch] = alpha[ch] * acc_ref[ch] + pv[ch]
            m_ref[ch] = m_new[ch]

    def park(s):
        for ch, s_ch in s.items():
            s_ref[ch] = s_ch

    own = [(b, [2 * a + h for a in range(b, nh) for h in range(2)], [2 * b, 2 * b + 1])
           for b in reversed(range(nh))]
    park(scores(base + own[0][0], own[0][1], own[0][2]))
    for (b_cur, ch_cur, _), (b_nxt, ch_nxt, mk_nxt) in zip(own[:-1], own[1:]):
        s_next = scores(base + b_nxt, ch_nxt, mk_nxt)
        accumulate(base + b_cur, ch_cur)
        park(s_next)

    def back(step, carry):
        s_next = scores(base - 1 - step, all_chains)
        accumulate(base - step, all_chains)
        park(s_next)
        return carry

    lax.fori_loop(0, n_prev, back, 0)
    accumulate(base - n_prev, all_chains)

    for a in range(nh):
        a0 = acc_ref[2 * a]
        a1 = acc_ref[2 * a + 1]
        o0 = a0 / a0[:, HEAD_DIM:HEAD_DIM + 1]
        o1 = a1 / a1[:, 0:1]
        o_ref[a * t:(a + 1) * t, :] = jnp.where(first, o0, o1).astype(o_ref.dtype)


def _attn_prompt(layer, q, kp, vp, lft):
    t = ATT_T
    nt = N_PROMPT // t
    tq = ATT_QH * t
    kv_spec = pl.BlockSpec((1, N_PROMPT, LANES), lambda j, i: (layer, 0, j))
    return pl.pallas_call(
        _attn_prompt_kernel,
        grid=(N_HEADS // 2, N_PROMPT // tq),
        in_specs=[
            pl.BlockSpec((tq, LANES), lambda j, i: (i, j)),
            kv_spec,
            kv_spec,
            pl.BlockSpec((1, 2, N_PROMPT), lambda j, i: (j, 0, 0)),
        ],
        out_specs=pl.BlockSpec((tq, LANES), lambda j, i: (i, j)),
        out_shape=jax.ShapeDtypeStruct((N_PROMPT, ATT_WIDTH), BF16),
        scratch_shapes=[
            pltpu.VMEM((2, N_PROMPT), F32),
            pltpu.VMEM((nt, 2, t), F32),
            pltpu.VMEM((2, 1), F32),
            pltpu.VMEM((N_PROMPT, LANES), BF16),
            pltpu.VMEM((N_PROMPT, LANES), BF16),
            pltpu.VMEM((N_PROMPT, LANES), BF16),
            pltpu.VMEM((2 * ATT_QH, t, LANES), F32),
            pltpu.VMEM((2 * ATT_QH, t, LANES), F32),
            pltpu.VMEM((2 * ATT_QH, t, t), F32),
        ],
        compiler_params=_cparams(("parallel", "arbitrary")),
        name="attn_prompt",
    )(q, kp, vp, lft)


SAMPLE_HEAD_GROUP = 4


def _attn_sample_kernel(q_ref, kn_ref, vn_ref, kc_ref, vc_ref, lfc_ref, lfn_ref, o_ref):
    t = q_ref.shape[0]
    q = q_ref[...]
    kn = kn_ref[0].astype(BF16)
    vn = vn_ref[0].astype(BF16)
    xc = lfc_ref[0]
    ckc = (xc - _lane_cumsum(xc, reverse=True)) * LOG2E
    ckn = _lane_cumsum(lfn_ref[0])[:, :t] * LOG2E
    keep = lax.broadcasted_iota(jnp.int32, (t, t), 1) <= lax.broadcasted_iota(jnp.int32, (t, t), 0)
    dn = (((1,), (1,)), ((), ()))
    outs = []
    for h0 in range(0, N_HEADS, SAMPLE_HEAD_GROUP):
        hs = range(h0, h0 + SAMPLE_HEAD_GROUP)
        sl = [slice(h * HEAD_DIM, (h + 1) * HEAD_DIM) for h in hs]
        kc = [kc_ref[0, 0, h].astype(BF16) for h in hs]
        vc = [vc_ref[0, 0, h].astype(BF16) for h in hs]
        s_c = [jnp.dot(q[:, s], k, preferred_element_type=F32) - ckc[h:h + 1, :]
               for h, s, k in zip(hs, sl, kc)]
        s_n = [jnp.where(keep, lax.dot_general(q[:, s], kn[:, s], dn, preferred_element_type=F32)
                         - ckn[h:h + 1, :], NEG_BIG) for h, s in zip(hs, sl)]
        m = [jnp.maximum(jnp.max(a, axis=-1, keepdims=True), jnp.max(b, axis=-1, keepdims=True))
             for a, b in zip(s_c, s_n)]
        p_c = [jnp.exp2(a - mm) for a, mm in zip(s_c, m)]
        p_n = [jnp.exp2(b - mm) for b, mm in zip(s_n, m)]
        l = [jnp.sum(a, axis=-1, keepdims=True) + jnp.sum(b, axis=-1, keepdims=True)
             for a, b in zip(p_c, p_n)]
        outs += [(lax.dot_general(a.astype(BF16), v, dn, preferred_element_type=F32)
                  + jnp.dot(b.astype(BF16), vn[:, s], preferred_element_type=F32)) / ll
                 for a, b, v, s, ll in zip(p_c, p_n, vc, sl, l)]
    o_ref[...] = jnp.concatenate(outs, axis=-1).astype(o_ref.dtype)


def _attn_sample(layer, q, ks, vs, cache_k, cache_v, lfc, lfn):
    t = DEC_SEQ
    new = lambda b: (layer, b, 0)
    cache_spec = pl.BlockSpec((1, 1, N_HEADS, HEAD_DIM, PAST_LEN), lambda b: (layer, b, 0, 0, 0))
    return pl.pallas_call(
        _attn_sample_kernel,
        grid=(DEC_BATCH,),
        in_specs=[
            pl.BlockSpec((t, ATT_WIDTH), lambda b: (N_PROMPT // t + b, 0)),
            pl.BlockSpec((1, t, ATT_WIDTH), new),
            pl.BlockSpec((1, t, ATT_WIDTH), new),
            cache_spec,
            cache_spec,
            pl.BlockSpec((1, N_HEADS, PAST_LEN), lambda b: (b, 0, 0)),
            pl.BlockSpec((1, N_HEADS, LANES), lambda b: (b, 0, 0)),
        ],
        out_specs=pl.BlockSpec((t, ATT_WIDTH), lambda b: (b, 0)),
        out_shape=jax.ShapeDtypeStruct((N_SAMPLE, ATT_WIDTH), BF16),
        compiler_params=_cparams(("parallel",)),
        name="attn_sample",
    )(q, ks, vs, cache_k, cache_v, lfc, lfn)


def _ssm_kernel(nseq, nblk, chained, u_ref, wst_ref, win_ref, wca_ref, a_ref, h0_ref, y_ref, hf_ref, sx_ref):
    nlb = 2 * SSM_S // LANES
    lanes = lambda x: [x[:, c * LANES:(c + 1) * LANES] for c in range(nlb)]
    u = u_ref[0]
    for c, blk in enumerate(lanes(jnp.dot(u, wst_ref[0], preferred_element_type=F32))):
        sx_ref[c] = blk
    a = lanes(a_ref[0])
    h0 = lanes(h0_ref[0])
    half = nlb // 2

    def step(a_, x):
        return ([a_[c] * x[c] - a_[c + half] * x[c + half] for c in range(half)]
                + [a_[c] * x[c + half] + a_[c + half] * x[c] for c in range(half)])

    def swap_in(rows, state):
        blk = [sx_ref[c, rows, :] for c in range(nlb)]
        for c in range(nlb):
            sx_ref[c, rows, :] = state[c]
        return blk

    rows_of = lambda j: pl.ds(pl.multiple_of(j * nseq, nseq), nseq)

    def body(j, state):
        blk = swap_in(rows_of(j), state)
        return [n + b for n, b in zip(step(a, state), blk)]

    if chained:
        ends = lax.fori_loop(0, nblk, body, [jnp.zeros((nseq, LANES), F32)] * nlb)
        a_seg = a
        for _ in range(nblk.bit_length() - 1):
            a_seg = step(a_seg, a_seg)
        state, starts = h0, []
        for s in range(nseq):
            starts.append(state)
            state = [n + e[s:s + 1] for n, e in zip(step(a_seg, state), ends)]
        final = state

        def fix(j, corr):
            for c in range(nlb):
                sx_ref[c, rows_of(j), :] = sx_ref[c, rows_of(j), :] + corr[c]
            return step(a, corr)

        lax.fori_loop(0, nblk, fix, [jnp.concatenate([st[c] for st in starts], axis=0) for c in range(nlb)])
    else:
        final = lax.fori_loop(0, nblk, body, h0)
    hf_ref[0] = jnp.concatenate(final, axis=-1)
    s_excl = jnp.concatenate([sx_ref[c] for c in range(nlb)], axis=-1).astype(BF16)
    y_ref[0] = (jnp.dot(u, win_ref[0], preferred_element_type=F32)
                + jnp.dot(s_excl, wca_ref[0], preferred_element_type=F32))


def _ssm(u, wst, win, wca, a, h0, nseq, nblk, chained):
    r = nblk * nseq
    nstate = 1 if chained else nseq
    if chained:
        assert nblk & (nblk - 1) == 0, nblk
    slab = lambda g: (g, 0, 0)
    return pl.pallas_call(
        functools.partial(_ssm_kernel, nseq, nblk, chained),
        grid=(N_GB,),
        in_specs=[
            pl.BlockSpec((1, r, SSM_K), slab),
            pl.BlockSpec((1, SSM_K, 2 * SSM_S), slab),
            pl.BlockSpec((1, SSM_K, SSM_K), slab),
            pl.BlockSpec((1, 2 * SSM_S, SSM_K), slab),
            pl.BlockSpec((1, 1, 2 * SSM_S), slab),
            pl.BlockSpec((1, nstate, 2 * SSM_S), slab),
        ],
        out_specs=[
            pl.BlockSpec((1, r, SSM_K), slab),
            pl.BlockSpec((1, nstate, 2 * SSM_S), slab),
        ],
        out_shape=(
            jax.ShapeDtypeStruct((N_GB, r, SSM_K), F32),
            jax.ShapeDtypeStruct((N_GB, nstate, 2 * SSM_S), F32),
        ),
        scratch_shapes=[pltpu.VMEM((2 * SSM_S // LANES, r, LANES), F32)],
        compiler_params=_cparams(("parallel",)),
        name="ssm_p" if chained else "ssm_s",
    )(u, wst, win, wca, a, h0)


def _ssm_weights(lam_re, lam_im, log_dt, b_re, b_im, c_re, c_im):
    hp = lax.Precision.HIGHEST
    nl = lam_re.shape[0]
    lb = SSM_LB
    dt = jnp.exp(log_dt.astype(F32))[..., None]
    lr, li = lam_re.astype(F32), lam_im.astype(F32)
    mag = jnp.exp(lr * dt)
    a_re = mag * jnp.cos(li * dt)
    a_im = mag * jnp.sin(li * dt)
    nr, ni = a_re - 1.0, a_im
    den = lr * lr + li * li
    f_re = (nr * lr + ni * li) / den
    f_im = (ni * lr - nr * li) / den
    br, bi = b_re.astype(F32), b_im.astype(F32)
    bb_re = f_re[..., None] * br - f_im[..., None] * bi
    bb_im = f_re[..., None] * bi + f_im[..., None] * br
    pr, pi = [jnp.ones_like(a_re)], [jnp.zeros_like(a_re)]
    for _ in range(lb):
        pr, pi = pr + [pr[-1] * a_re - pi[-1] * a_im], pi + [pr[-1] * a_im + pi[-1] * a_re]
    p_re, p_im = jnp.stack(pr), jnp.stack(pi)
    xb_re = p_re[:lb, ..., None] * bb_re - p_im[:lb, ..., None] * bb_im
    xb_im = p_re[:lb, ..., None] * bb_im + p_im[:lb, ..., None] * bb_re
    cr, ci = c_re.astype(F32), c_im.astype(F32)
    kt = (jnp.einsum('lgmp,tlgpn->tlgmn', cr, xb_re, precision=hp)
          - jnp.einsum('lgmp,tlgpn->tlgmn', ci, xb_im, precision=hp))
    ktc = kt.reshape(lb, nl, N_GB, SSM_GB, SSM_GROUP, SSM_GROUP).transpose(1, 2, 0, 3, 5, 4)
    xs = jnp.stack([xb_re[::-1], xb_im[::-1]], axis=1)
    xsc = xs.reshape(lb, 2, nl, N_GB, SSM_GB, STATE_DIM, SSM_GROUP).transpose(2, 3, 0, 1, 4, 6, 5)
    m_re = cr[None] * p_re[1:, :, :, None, :] - ci[None] * p_im[1:, :, :, None, :]
    m_im = cr[None] * p_im[1:, :, :, None, :] + ci[None] * p_re[1:, :, :, None, :]
    ms = jnp.stack([m_re, -m_im], axis=0)
    mcc = ms.reshape(2, lb, nl, N_GB, SSM_GB, SSM_GROUP, STATE_DIM).transpose(2, 3, 0, 1, 4, 6, 5)
    win, wst, wca = _ssm_expand(ktc, xsc, mcc)
    a_lb = jnp.concatenate([p_re[lb].reshape(nl, N_GB, 1, SSM_S),
                            p_im[lb].reshape(nl, N_GB, 1, SSM_S)], axis=-1)
    return wst, win, wca, a_lb


def _ssm_expand_kernel(kt_ref, xs_ref, mc_ref, win_ref, wst_ref, wca_ref, w_ref, bd_ref):
    lb, gb, grp, sd = SSM_LB, SSM_GB, SSM_GROUP, STATE_DIM
    bd_ref[...] = jnp.zeros(bd_ref.shape, F32)
    for tau in range(lb):
        for g in range(gb):
            bd_ref[tau, g * grp:(g + 1) * grp, g * grp:(g + 1) * grp] = kt_ref[0, 0, tau, g]
    w_ref[...] = jnp.zeros(w_ref.shape, F32)
    for r0 in range(lb):
        for r1 in range(r0, lb):
            w_ref[r0 * LANES:(r0 + 1) * LANES, r1 * LANES:(r1 + 1) * LANES] = bd_ref[r1 - r0]
    win_ref[0, 0] = w_ref[...].astype(BF16)

    w_ref[...] = jnp.zeros(w_ref.shape, F32)
    for r0 in range(lb):
        for ri in range(2):
            for g in range(gb):
                rows = slice(r0 * LANES + g * grp, r0 * LANES + (g + 1) * grp)
                cols = slice(ri * SSM_S + g * sd, ri * SSM_S + (g + 1) * sd)
                w_ref[rows, cols] = xs_ref[0, 0, r0, ri, g]
    wst_ref[0, 0] = w_ref[...].astype(BF16)

    w_ref[...] = jnp.zeros(w_ref.shape, F32)
    for ri in range(2):
        for r1 in range(lb):
            for g in range(gb):
                rows = slice(ri * SSM_S + g * sd, ri * SSM_S + (g + 1) * sd)
                cols = slice(r1 * LANES + g * grp, r1 * LANES + (g + 1) * grp)
                w_ref[rows, cols] = mc_ref[0, 0, ri, r1, g]
    wca_ref[0, 0] = w_ref[...].astype(BF16)


def _ssm_expand(ktc, xsc, mcc):
    nl = ktc.shape[0]
    lb, gb, grp, sd = SSM_LB, SSM_GB, SSM_GROUP, STATE_DIM
    slab = lambda l, b: (l, b, 0, 0)
    out = jax.ShapeDtypeStruct((nl, N_GB, SSM_K, SSM_K), BF16)
    return pl.pallas_call(
        _ssm_expand_kernel,
        grid=(nl, N_GB),
        in_specs=[
            pl.BlockSpec((1, 1, lb, gb, grp, grp), lambda l, b: (l, b, 0, 0, 0, 0)),
            pl.BlockSpec((1, 1, lb, 2, gb, grp, sd), lambda l, b: (l, b, 0, 0, 0, 0, 0)),
            pl.BlockSpec((1, 1, 2, lb, gb, sd, grp), lambda l, b: (l, b, 0, 0, 0, 0, 0)),
        ],
        out_specs=[pl.BlockSpec((1, 1, SSM_K, SSM_K), slab)] * 3,
        out_shape=(out, out, out),
        scratch_shapes=[pltpu.VMEM((SSM_K, SSM_K), F32), pltpu.VMEM((lb, LANES, LANES), F32)],
        compiler_params=_cparams(("parallel", "parallel")),
        name="ssm_expand",
    )(ktc, xsc, mcc)


def _merge_kernel(x_ref, oatt_ref, y_ref, s_ref, gate_ref, d_ref, wglu_ref, bglu_ref,
                  pa_ref, pb_ref, wo_ref, gpost_ref, o_ref):
    y = jax.nn.gelu(y_ref[...] + d_ref[...] * s_ref[...])
    glu = jnp.dot(y.astype(BF16), wglu_ref[...], preferred_element_type=F32) + bglu_ref[...]
    o_ssm = (y * jax.nn.sigmoid(glu)).astype(BF16)
    pa = jnp.dot(oatt_ref[...], pa_ref[...], preferred_element_type=F32)
    pb = jnp.dot(o_ssm, pb_ref[...], preferred_element_type=F32)
    gate = gate_ref[...].astype(F32)
    merged = gate[:, :D_MODEL] * pa + gate[:, D_MODEL:] * pb
    m = jnp.dot(merged.astype(BF16), wo_ref[...], preferred_element_type=F32)
    o_ref[...] = x_ref[...] + _rms(m, gpost_ref[...])


def _merge(x, o_att, y_raw, s_in, gate, d, wglu, bglu, pa, pb, wo, gpost):
    n = x.shape[0]
    tm = MERGE_TM
    row = lambda i: (i, 0)
    const = lambda i: (0, 0)
    once = pl.Buffered(1)
    return pl.pallas_call(
        _merge_kernel,
        grid=(n // tm,),
        in_specs=[
            pl.BlockSpec((tm, D_MODEL), row),
            pl.BlockSpec((tm, ATT_WIDTH), row),
            pl.BlockSpec((tm, SSM_WIDTH), row),
            pl.BlockSpec((tm, SSM_WIDTH), row),
            pl.BlockSpec((tm, 2 * D_MODEL), row),
            pl.BlockSpec((1, SSM_WIDTH), const),
            pl.BlockSpec((SSM_WIDTH, SSM_WIDTH), const, pipeline_mode=once),
            pl.BlockSpec((1, SSM_WIDTH), const),
            pl.BlockSpec((ATT_WIDTH, D_MODEL), const, pipeline_mode=once),
            pl.BlockSpec((SSM_WIDTH, D_MODEL), const, pipeline_mode=once),
            pl.BlockSpec((D_MODEL, D_MODEL), const, pipeline_mode=once),
            pl.BlockSpec((1, D_MODEL), const),
        ],
        out_specs=pl.BlockSpec((tm, D_MODEL), row),
        out_shape=jax.ShapeDtypeStruct((n, D_MODEL), F32),
        compiler_params=_cparams(("parallel",)),
        name="merge",
    )(x, o_att, y_raw, s_in, gate, d, wglu, bglu, pa, pb, wo, gpost)


def _to_slabs(s, nseq, nblk):
    t = s.reshape(nseq, nblk, SSM_LB, N_GB, LANES)
    return t.transpose(3, 1, 0, 2, 4).reshape(N_GB, nblk * nseq, SSM_K)


def _from_slabs(y, nseq, nblk):
    t = y.reshape(N_GB, nblk, nseq, SSM_LB, LANES)
    return t.transpose(2, 1, 3, 0, 4).reshape(nseq * nblk * SSM_LB, SSM_WIDTH)


def _state_to_slabs(h_re, h_im):
    nseq = h_re.shape[0]
    r = h_re.astype(F32).reshape(nseq, N_GB, SSM_S).transpose(1, 0, 2)
    i = h_im.astype(F32).reshape(nseq, N_GB, SSM_S).transpose(1, 0, 2)
    return jnp.concatenate([r, i], axis=-1)


def _state_from_slabs(h):
    nseq = h.shape[1]
    r = h[:, :, :SSM_S].transpose(1, 0, 2).reshape(nseq, N_GROUPS, STATE_DIM)
    i = h[:, :, SSM_S:].transpose(1, 0, 2).reshape(nseq, N_GROUPS, STATE_DIM)
    return r, i


def _pad_cols(w, n):
    return jnp.pad(w, ((0, 0), (0, 0), (0, n - w.shape[-1])))


def kernel(x_prompt, x_sample, cache_k, cache_v, cache_logf, state_ssm_re, state_ssm_im, norm_f1_pre, norm_f1_post, norm_mix_pre, norm_mix_post, norm_f2_pre, norm_f2_post, ffn1_w_gate, ffn1_w_up, ffn1_w_down, ffn2_w_gate, ffn2_w_up, ffn2_w_down, w_in, b_forget, b_gate, ssm_lam_re, ssm_lam_im, ssm_log_dt, ssm_b_re, ssm_b_im, ssm_c_re, ssm_c_im, ssm_d, ssm_w_glu, ssm_b_glu, w_proj_attn, w_proj_ssm, w_out):
    ffn_w = []
    for wg, wu, wd in ((ffn1_w_gate, ffn1_w_up, ffn1_w_down), (ffn2_w_gate, ffn2_w_up, ffn2_w_down)):
        ffn_w.append((_cast_pad_cols(wg), _cast_pad_cols(wu),
                      jnp.pad(wd.astype(BF16), ((0, 0), (0, D_FF_PAD - D_FF), (0, 0)))))
    a3 = 3 * ATT_WIDTH
    w_main = jnp.concatenate([w_in[:, :, :a3], w_in[:, :, a3 + N_HEADS:]], axis=-1).astype(BF16)
    w_f = _pad_cols(w_in[:, :, a3:a3 + N_HEADS].astype(BF16), LANES)
    b_f = _pad_cols(b_forget.astype(F32)[:, None, :], LANES)
    wglu = ssm_w_glu.astype(BF16)
    wpa = w_proj_attn.astype(BF16)
    wpb = w_proj_ssm.astype(BF16)
    wo = w_out.astype(BF16)
    wst, win, wca, a_lb = _ssm_weights(ssm_lam_re, ssm_lam_im, ssm_log_dt, ssm_b_re, ssm_b_im,
                                       ssm_c_re, ssm_c_im)
    row = lambda v: v.astype(F32)[:, None, :]
    g_f1a, g_f1b, g_ma, g_mb, g_f2a, g_f2b = map(row, (norm_f1_pre, norm_f1_post, norm_mix_pre,
                                                       norm_mix_post, norm_f2_pre, norm_f2_post))
    b_g, d_ssm, b_glu = row(b_gate), row(ssm_d), row(ssm_b_glu)

    x = (x_prompt.reshape(N_PROMPT, D_MODEL), x_sample.reshape(N_SAMPLE, D_MODEL))
    nblk_p = N_PROMPT // SSM_LB // SSM_SEG
    nblk_s = DEC_SEQ // SSM_LB
    h0_p = jnp.zeros((N_GB, 1, 2 * SSM_S), F32)
    lfc_all = cache_logf.astype(F32).transpose(0, 1, 3, 2)
    cache_k2 = cache_k.transpose(0, 1, 3, 4, 2)
    cache_v2 = cache_v.transpose(0, 1, 3, 4, 2)
    kp = jnp.zeros((DEPTH, N_PROMPT, ATT_WIDTH), F32)
    vp = jnp.zeros((DEPTH, N_PROMPT, ATT_WIDTH), F32)
    ks = jnp.zeros((DEPTH, N_SAMPLE, ATT_WIDTH), F32)
    vs = jnp.zeros((DEPTH, N_SAMPLE, ATT_WIDTH), F32)
    outs = [[] for _ in range(6)]
    for l in range(DEPTH):
        x = _ffn(x, g_f1a[l], g_f1b[l], *[w[l] for w in ffn_w[0]])
        q, s_in, gate, lf, kp, vp, ks, vs = _inproj(
            l, x, g_ma[l], w_main[l], w_f[l], b_f[l], b_g[l], kp, vp, ks, vs)
        logf = lf[:, :N_HEADS]
        logf_p = logf[:N_PROMPT]
        logf_s = logf[N_PROMPT:].reshape(DEC_BATCH, DEC_SEQ, N_HEADS)

        o_p = _attn_prompt(l, q, kp, vp, logf_p.T.reshape(N_HEADS // 2, 2, N_PROMPT))
        lfn = jnp.pad(logf_s.transpose(0, 2, 1), ((0, 0), (0, 0), (0, LANES - DEC_SEQ)))
        o_s = _attn_sample(l, q, ks, vs, cache_k2, cache_v2, lfc_all[l], lfn)
        o_att = jnp.concatenate([o_p, o_s], axis=0)

        u = s_in.astype(BF16)
        y_p, h_p = _ssm(_to_slabs(u[:N_PROMPT], SSM_SEG, nblk_p), wst[l], win[l], wca[l], a_lb[l],
                        h0_p, SSM_SEG, nblk_p, True)
        h0_s = _state_to_slabs(state_ssm_re[l], state_ssm_im[l])
        y_s, h_s = _ssm(_to_slabs(u[N_PROMPT:], DEC_BATCH, nblk_s), wst[l], win[l], wca[l], a_lb[l],
                        h0_s, DEC_BATCH, nblk_s, False)
        y_raw = jnp.concatenate([_from_slabs(y_p, SSM_SEG, nblk_p), _from_slabs(y_s, DEC_BATCH, nblk_s)], axis=0)

        x = _merge(x, o_att, y_raw, s_in, gate, d_ssm[l], wglu[l], b_glu[l], wpa[l], wpb[l], wo[l], g_mb[l])
        x = _ffn(x, g_f2a[l], g_f2b[l], *[w[l] for w in ffn_w[1]], split_out=(l == DEPTH - 1))

        hp_re, hp_im = _state_from_slabs(h_p)
        hs_re, hs_im = _state_from_slabs(h_s)
        for o, v in zip(outs, (logf_p.reshape(1, SEQ, N_HEADS), hp_re, hp_im, logf_s, hs_re, hs_im)):
            o.append(v)
    lf_p, hr_p, hi_p, lf_s, hr_s, hi_s = [jnp.stack(o) for o in outs]
    return (x[0].reshape(1, SEQ, D_MODEL), x[1].reshape(DEC_BATCH, DEC_SEQ, D_MODEL),
            kp.reshape(DEPTH, 1, SEQ, N_HEADS, HEAD_DIM), vp.reshape(DEPTH, 1, SEQ, N_HEADS, HEAD_DIM),
            lf_p, hr_p, hi_p,
            ks.reshape(DEPTH, DEC_BATCH, DEC_SEQ, N_HEADS, HEAD_DIM),
            vs.reshape(DEPTH, DEC_BATCH, DEC_SEQ, N_HEADS, HEAD_DIM),
            lf_s, hr_s, hi_s)
```
